```python
import math
import jax, jax.numpy as jnp
from jax import lax
import numpy as np

D_MODEL = 2048
BATCH = 4
SEQ = 4096
DEPTH = 4

D_MIX = D_MODEL
D_CONV = D_MIX // 4
CONV_GROUPS = 4
CONV_WIDTH = 3
HEAD_DIM = 128
D_ATT = D_MIX // 2
N_ATT_HEADS = D_ATT // HEAD_DIM
D_LRU = D_MIX - D_CONV - D_ATT
LRU_BLOCKS = 4
LRU_BLOCK = D_LRU // LRU_BLOCKS
LRU_CONV_WIDTH = 4
LRU_C = 8.0
D_IN = 3 * D_CONV + 3 * D_ATT + N_ATT_HEADS + 2 * D_LRU
D_FF = 256 * int(math.ceil(8 * D_MODEL / 3 / 256))
Q_BLOCK = 128
EPS = 1e-6

kernel_name = "hymba_parallel_conv_fox_rglru_macaron"


def rmsnorm(x, g):
    xf = x.astype(jnp.float32)
    y = xf * lax.rsqrt(jnp.mean(xf * xf, axis=-1, keepdims=True) + EPS)
    return (y * g.astype(jnp.float32)).astype(x.dtype)


def swiglu(h, w_in, w_out):
    g, u = jnp.split(h @ w_in, 2, axis=-1)
    return (jax.nn.silu(g) * u) @ w_out


def causal_depthwise_conv(x, w):
    k_width, ch = w.shape
    return lax.conv_general_dilated(
        x, w[:, None, :].astype(x.dtype), window_strides=(1,),
        padding=[(k_width - 1, 0)], dimension_numbers=("NWC", "WIO", "NWC"),
        feature_group_count=ch)


def forgetting_attention(q, k, v, f_logit, b_f):
    bsz, seq, _ = q.shape
    def heads(t):
        return t.reshape(bsz, seq, N_ATT_HEADS, HEAD_DIM).transpose(0, 2, 1, 3)
    q, k, v = heads(q), heads(k), heads(v)
    log_f = jax.nn.log_sigmoid(f_logit.astype(jnp.float32) + b_f.astype(jnp.float32))
    cum = jnp.cumsum(log_f, axis=1).transpose(0, 2, 1)
    scale = HEAD_DIM ** -0.5
    k_pos = jnp.arange(seq)

    def block(i):
        start = i * Q_BLOCK
        qb = lax.dynamic_slice_in_dim(q, start, Q_BLOCK, axis=2)
        cb = lax.dynamic_slice_in_dim(cum, start, Q_BLOCK, axis=2)
        s = (jnp.einsum("bhqd,bhkd->bhqk", qb, k).astype(jnp.float32) * scale
             + cb[..., None] - cum[:, :, None, :])
        q_pos = start + jnp.arange(Q_BLOCK)
        s = jnp.where(k_pos[None, :] <= q_pos[:, None], s, -jnp.inf)
        p = jax.nn.softmax(s, axis=-1)
        return jnp.einsum("bhqk,bhkd->bhqd", p.astype(v.dtype), v)

    o = lax.map(block, jnp.arange(seq // Q_BLOCK))
    return o.transpose(1, 0, 3, 2, 4).reshape(bsz, seq, D_ATT)


def rg_lru(x, w_a, b_a, w_x, b_x, lam):
    bsz, seq, _ = x.shape
    xb = x.reshape(bsz, seq, LRU_BLOCKS, LRU_BLOCK)
    r = jax.nn.sigmoid(jnp.einsum("btgi,gij->btgj", xb, w_a).reshape(bsz, seq, D_LRU) + b_a)
    i = jax.nn.sigmoid(jnp.einsum("btgi,gij->btgj", xb, w_x).reshape(bsz, seq, D_LRU) + b_x)
    log_a = -LRU_C * r.astype(jnp.float32) * jax.nn.softplus(-lam.astype(jnp.float32))
    a = jnp.exp(log_a)
    u = jnp.sqrt(-jnp.expm1(2.0 * log_a)) * (i * x).astype(jnp.float32)

    def combine(left, right):
        a1, b1 = left
        a2, b2 = right
        return a1 * a2, a2 * b1 + b2

    _, h = lax.associative_scan(combine, (a, u), axis=1)
    return h.astype(x.dtype)


def parallel_mixer(h, w_in, conv_w, fgate_b, lru_conv_w, lru_conv_b, lru_w_a, lru_b_a,
                   lru_w_x, lru_b_x, lru_lambda, out_norm, w_out):
    sizes = [D_CONV] * 3 + [D_ATT] * 3 + [N_ATT_HEADS] + [D_LRU] * 2
    splits = [int(s) for s in np.cumsum(sizes)[:-1]]
    c_b, c_c, c_v, q, k, v, f_logit, lru_gate, lru_x = jnp.split(h @ w_in, splits, axis=-1)
    y_conv = c_b * causal_depthwise_conv(c_c * c_v, conv_w)
    y_att = forgetting_attention(q, k, v, f_logit, fgate_b)
    xr = causal_depthwise_conv(lru_x, lru_conv_w) + lru_conv_b
    y_lru = jax.nn.gelu(lru_gate) * rg_lru(xr, lru_w_a, lru_b_a, lru_w_x, lru_b_x, lru_lambda)
    g_c, g_a, g_l = jnp.split(out_norm, [D_CONV, D_CONV + D_ATT])
    y = jnp.concatenate([rmsnorm(y_conv, g_c), rmsnorm(y_att, g_a), rmsnorm(y_lru, g_l)], axis=-1)
    return y @ w_out


def setup_inputs(seed: int = 0) -> dict:
    key = jax.random.key(seed)
    ks = iter(jax.random.split(key, 32))
    f32 = jnp.float32
    res_scale = (2.0 * DEPTH) ** -0.5

    def normal(shape, std):
        return jax.random.normal(next(ks), shape, f32) * std

    def gain(shape):
        return 1.0 + normal(shape, 0.02)

    x = jax.random.normal(next(ks), (BATCH, SEQ, D_MODEL), f32)
    a_c = jax.random.uniform(next(ks), (DEPTH, D_LRU), f32, 0.9, 0.999)
    s = a_c ** (1.0 / LRU_C)
    lru_lambda = jnp.log(s) - jnp.log1p(-s)
    return {
        "x": x,
        "norm_ffn1": gain((DEPTH, D_MODEL)),
        "ffn1_w_in": normal((DEPTH, D_MODEL, 2 * D_FF), D_MODEL ** -0.5),
        "ffn1_w_out": normal((DEPTH, D_FF, D_MODEL), D_FF ** -0.5 * res_scale),
        "norm_mix": gain((DEPTH, D_MODEL)),
        "mix_w_in": normal((DEPTH, D_MODEL, D_IN), D_MODEL ** -0.5),
        "conv_w": normal((DEPTH, CONV_WIDTH, D_CONV), CONV_WIDTH ** -0.5),
        "fgate_b": jax.random.uniform(next(ks), (DEPTH, N_ATT_HEADS), f32, 1.0, 4.0),
        "lru_conv_w": normal((DEPTH, LRU_CONV_WIDTH, D_LRU), LRU_CONV_WIDTH ** -0.5),
        "lru_conv_b": normal((DEPTH, D_LRU), 0.02),
        "lru_w_a": normal((DEPTH, LRU_BLOCKS, LRU_BLOCK, LRU_BLOCK), LRU_BLOCK ** -0.5),
        "lru_b_a": normal((DEPTH, D_LRU), 0.02),
        "lru_w_x": normal((DEPTH, LRU_BLOCKS, LRU_BLOCK, LRU_BLOCK), LRU_BLOCK ** -0.5),
        "lru_b_x": normal((DEPTH, D_LRU), 0.02),
        "lru_lambda": lru_lambda,
        "mix_out_norm": gain((DEPTH, D_MIX)),
        "mix_w_out": normal((DEPTH, D_MIX, D_MODEL), D_MIX ** -0.5 * res_scale),
        "norm_ffn2": gain((DEPTH, D_MODEL)),
        "ffn2_w_in": normal((DEPTH, D_MODEL, 2 * D_FF), D_MODEL ** -0.5),
        "ffn2_w_out": normal((DEPTH, D_FF, D_MODEL), D_FF ** -0.5 * res_scale),
        "final_norm": gain((D_MODEL,)),
    }


def reference(x, norm_ffn1, ffn1_w_in, ffn1_w_out, norm_mix, mix_w_in, conv_w, fgate_b,
              lru_conv_w, lru_conv_b, lru_w_a, lru_b_a, lru_w_x, lru_b_x, lru_lambda,
              mix_out_norm, mix_w_out, norm_ffn2, ffn2_w_in, ffn2_w_out, final_norm):
    for l in range(DEPTH):
        x = x + 0.5 * swiglu(rmsnorm(x, norm_ffn1[l]), ffn1_w_in[l], ffn1_w_out[l])
        x = x + parallel_mixer(rmsnorm(x, norm_mix[l]), mix_w_in[l], conv_w[l], fgate_b[l],
                               lru_conv_w[l], lru_conv_b[l], lru_w_a[l], lru_b_a[l],
                               lru_w_x[l], lru_b_x[l], lru_lambda[l], mix_out_norm[l],
                               mix_w_out[l])
        x = x + 0.5 * swiglu(rmsnorm(x, norm_ffn2[l]), ffn2_w_in[l], ffn2_w_out[l])
    return rmsnorm(x, final_norm)
```

```python
import functools

import jax
import jax.numpy as jnp
from jax import lax
from jax.experimental import pallas as pl
from jax.experimental.pallas import tpu as pltpu

F32 = jnp.float32
BF16 = jnp.bfloat16

EPS = 1e-6
HEAD_DIM = 128
N_ATT_HEADS = 8
LRU_BLOCKS = 4
LRU_C = 8.0
CONV_WIDTH = 3
LRU_CONV_WIDTH = 4
LANES = 128
SUBLANES = 8
VMEM_LIMIT = 56 * 1024 * 1024


def _params(n_axes, vmem=VMEM_LIMIT):
    return pltpu.CompilerParams(dimension_semantics=("arbitrary",) * n_axes,
                                vmem_limit_bytes=vmem)


def _rms(x, gain):
    ms = jnp.mean(x * x, axis=-1, keepdims=True)
    return (x * lax.rsqrt(ms + EPS)) * gain


def _ffn_kernel(x_ref, g_ref, wg_ref, wu_ref, wo_ref, o_ref, xn_ref, acc_ref):
    k = pl.program_id(1)

    @pl.when(k == 0)
    def _():
        xn_ref[...] = _rms(x_ref[...], g_ref[...]).astype(BF16)
        acc_ref[...] = jnp.zeros_like(acc_ref)

    xn = xn_ref[...]
    g = jnp.dot(xn, wg_ref[...], preferred_element_type=F32)
    u = jnp.dot(xn, wu_ref[...], preferred_element_type=F32)
    h = ((g * jax.nn.sigmoid(g)) * u).astype(BF16)
    acc_ref[...] += jnp.dot(h, wo_ref[...], preferred_element_type=F32)

    @pl.when(k == pl.num_programs(1) - 1)
    def _():
        o_ref[...] = x_ref[...] + 0.5 * acc_ref[...]


def _ffn(x, gain, w_in, w_out, tm=512, tf=512):
    n, d = x.shape
    f = w_out.shape[0]
    nk = f // tf
    return pl.pallas_call(
        _ffn_kernel,
        grid=(n // tm, nk),
        in_specs=[
            pl.BlockSpec((tm, d), lambda i, k: (i, 0)),
            pl.BlockSpec((1, d), lambda i, k: (0, 0)),
            pl.BlockSpec((d, tf), lambda i, k: (0, k)),
            pl.BlockSpec((d, tf), lambda i, k: (0, k + nk)),
            pl.BlockSpec((tf, d), lambda i, k: (k, 0)),
        ],
        out_specs=pl.BlockSpec((tm, d), lambda i, k: (i, 0)),
        out_shape=jax.ShapeDtypeStruct((n, d), F32),
        scratch_shapes=[pltpu.VMEM((tm, d), BF16), pltpu.VMEM((tm, d), F32)],
        compiler_params=_params(2),
        name="ffn",
    )(x, gain.reshape(1, d), w_in, w_in, w_out)


def _mix_in_kernel(x_ref, g_ref, w_ref, conv_ref, q_ref, k_ref, v_ref, lru_ref, f_ref, *, chunk):
    xn = _rms(x_ref[...], g_ref[...]).astype(BF16)
    col = 0
    for out in (conv_ref, q_ref, k_ref, v_ref, lru_ref, f_ref):
        width = out.shape[1]
        for c in range(0, width, chunk):
            w = min(chunk, width - c)
            y = jnp.dot(xn, w_ref[:, col + c:col + c + w], preferred_element_type=F32)
            out[:, c:c + w] = y.astype(out.dtype)
        col += width


def _mix_in(x, gain, w, d_conv, d_att, d_lru, tm=256, chunk=512):
    n, d = x.shape
    widths = (3 * d_conv, d_att, d_att, d_att, 2 * d_lru, LANES)
    dtypes = (F32, BF16, BF16, BF16, F32, F32)
    assert w.shape == (d, sum(widths))
    return pl.pallas_call(
        functools.partial(_mix_in_kernel, chunk=chunk),
        grid=(n // tm,),
        in_specs=[
            pl.BlockSpec((tm, d), lambda i: (i, 0)),
            pl.BlockSpec((1, d), lambda i: (0, 0)),
            pl.BlockSpec(w.shape, lambda i: (0, 0), pipeline_mode=pl.Buffered(1)),
        ],
        out_specs=[pl.BlockSpec((tm, wd), lambda i: (i, 0)) for wd in widths],
        out_shape=[jax.ShapeDtypeStruct((n, wd), dt) for wd, dt in zip(widths, dtypes)],
        compiler_params=_params(1),
        name="mix_in",
    )(x, gain.reshape(1, d), w)


def _fcum_kernel(f_ref, b_ref, col_ref, row_ref, carry_ref):
    @pl.when(pl.program_id(1) == 0)
    def _():
        carry_ref[...] = jnp.zeros_like(carry_ref)

    z = f_ref[...] + b_ref[...]
    log_f = -(jnp.maximum(-z, 0.0) + jnp.log1p(jnp.exp(-jnp.abs(z))))
    t = log_f.shape[0]
    r = lax.broadcasted_iota(jnp.int32, (t, t), 0)
    c = lax.broadcasted_iota(jnp.int32, (t, t), 1)
    tri = (c <= r).astype(F32)
    cum = jnp.dot(tri, log_f, preferred_element_type=F32,
                  precision=lax.Precision.HIGHEST) + carry_ref[0:1, :]
    col_ref[...] = cum
    carry_ref[...] = jnp.broadcast_to(cum[t - 1:t, :], carry_ref.shape)
    row_ref[0] = cum.T[0:SUBLANES, :]


def _fcum(f, bias, batch, seq, tc=256):
    n = f.shape[0]
    nt = seq // tc
    return pl.pallas_call(
        _fcum_kernel,
        grid=(batch, nt),
        in_specs=[
            pl.BlockSpec((tc, LANES), lambda b, t: (b * nt + t, 0)),
            pl.BlockSpec((1, LANES), lambda b, t: (0, 0)),
        ],
        out_specs=[
            pl.BlockSpec((tc, LANES), lambda b, t: (b * nt + t, 0)),
            pl.BlockSpec((1, SUBLANES, tc), lambda b, t: (b, 0, t)),
        ],
        out_shape=[jax.ShapeDtypeStruct((n, LANES), F32),
                   jax.ShapeDtypeStruct((batch, SUBLANES, seq), F32)],
        scratch_shapes=[pltpu.VMEM((SUBLANES, LANES), F32)],
        compiler_params=_params(2),
        name="fcum",
    )(f, bias)


def _attn_kernel(q_ref, k_ref, v_ref, ccol_ref, crow_ref, o_ref, *, tq, scale):
    h = pl.program_id(1)
    i = pl.program_id(2)
    q = q_ref[...]
    lane = lax.broadcasted_iota(jnp.int32, ccol_ref.shape, 1)
    cq = jnp.sum(jnp.where(lane == h, ccol_ref[...], 0.0), axis=1, keepdims=True)

    def scores(j):
        start = pl.multiple_of(j * tq, tq)
        kj = k_ref[pl.ds(start, tq), :]
        s = lax.dot_general(q, kj, (((1,), (1,)), ((), ())), preferred_element_type=F32) * scale
        ck = crow_ref[0, :, pl.ds(start, tq)]
        return s + (cq - ck), start

    def update(s, start, carry):
        m, l, acc = carry
        m_new = jnp.maximum(m, jnp.max(s, axis=1, keepdims=True))
        alpha = jnp.exp(m - m_new)
        p = jnp.exp(s - m_new)
        l = alpha * l + jnp.sum(p, axis=1, keepdims=True)
        vj = v_ref[pl.ds(start, tq), :]
        acc = alpha * acc + jnp.dot(p.astype(BF16), vj, preferred_element_type=F32)
        return m_new, l, acc

    def body(j, carry):
        s, start = scores(j)
        return update(s, start, carry)

    init = (jnp.full((tq, 1), -jnp.inf, F32), jnp.zeros((tq, 1), F32),
            jnp.zeros((tq, q.shape[1]), F32))
    carry = lax.fori_loop(0, i, body, init)
    s, start = scores(i)
    r = lax.broadcasted_iota(jnp.int32, s.shape, 0)
    c = lax.broadcasted_iota(jnp.int32, s.shape, 1)
    s = jnp.where(c <= r, s, -jnp.inf)
    m, l, acc = update(s, start, carry)
    o_ref[...] = acc * (1.0 / l)


def _attention(q, k, v, ccol, crow, batch, seq, tq=256):
    n, d_att = q.shape
    nh = d_att // HEAD_DIM
    nq = seq // tq
    crow = crow.reshape(batch * SUBLANES, 1, seq)
    return pl.pallas_call(
        functools.partial(_attn_kernel, tq=tq, scale=HEAD_DIM ** -0.5),
        grid=(batch, nh, nq),
        in_specs=[
            pl.BlockSpec((tq, HEAD_DIM), lambda b, h, i: (b * nq + i, h)),
            pl.BlockSpec((seq, HEAD_DIM), lambda b, h, i: (b, h)),
            pl.BlockSpec((seq, HEAD_DIM), lambda b, h, i: (b, h)),
            pl.BlockSpec((tq, LANES), lambda b, h, i: (b * nq + i, 0)),
            pl.BlockSpec((1, 1, seq), lambda b, h, i: (b * SUBLANES + h, 0, 0)),
        ],
        out_specs=pl.BlockSpec((tq, HEAD_DIM), lambda b, h, i: (b * nq + i, h)),
        out_shape=jax.ShapeDtypeStruct((n, d_att), F32),
        compiler_params=_params(3),
        name="fox_attention",
    )(q, k, v, ccol, crow)


def _gelu_tanh(x):
    return 0.5 * x * (1.0 + jnp.tanh(0.7978845608028654 * (x + 0.044715 * (x * x * x))))


def _conv_lru_kernel(conv_ref, lru_ref, cw_ref, lw_ref, lb_ref, wa_ref, ba_ref, wx_ref, bx_ref,
                     lam_ref, yc_ref, yl_ref, zbuf, xbuf, abuf, ubuf, h_ref, *, tt, dc, dl, pad):
    @pl.when(pl.program_id(1) == 0)
    def _():
        zbuf[0:SUBLANES, :] = jnp.zeros((SUBLANES, dc), F32)
        xbuf[0:SUBLANES, :] = jnp.zeros((SUBLANES, dl), F32)
        h_ref[...] = jnp.zeros_like(h_ref)
        abuf[0:pad, :] = jnp.ones((pad, dl), F32)
        ubuf[0:pad, :] = jnp.zeros((pad, dl), F32)

    o = SUBLANES
    zbuf[o:o + tt, :] = conv_ref[:, dc:2 * dc] * conv_ref[:, 2 * dc:3 * dc]
    yc = cw_ref[CONV_WIDTH - 1:CONV_WIDTH, :] * zbuf[o:o + tt, :]
    for s in range(1, CONV_WIDTH):
        yc = yc + cw_ref[CONV_WIDTH - 1 - s:CONV_WIDTH - s, :] * zbuf[o - s:o - s + tt, :]
    yc_ref[...] = conv_ref[:, 0:dc] * yc
    zbuf[0:o, :] = zbuf[tt:tt + o, :]

    xbuf[o:o + tt, :] = lru_ref[:, dl:2 * dl]
    xr = lw_ref[LRU_CONV_WIDTH - 1:LRU_CONV_WIDTH, :] * xbuf[o:o + tt, :]
    for s in range(1, LRU_CONV_WIDTH):
        xr = xr + lw_ref[LRU_CONV_WIDTH - 1 - s:LRU_CONV_WIDTH - s, :] * xbuf[o - s:o - s + tt, :]
    xr = xr + lb_ref[...]
    xbuf[0:o, :] = xbuf[tt:tt + o, :]

    xb = xr.astype(BF16)
    blk = dl // LRU_BLOCKS
    lam = lam_ref[...]
    softplus_neg_lam = jnp.maximum(-lam, 0.0) + jnp.log1p(jnp.exp(-jnp.abs(lam)))
    for g in range(LRU_BLOCKS):
        sl = slice(g * blk, (g + 1) * blk)
        r = jax.nn.sigmoid(jnp.dot(xb[:, sl], wa_ref[g], preferred_element_type=F32) + ba_ref[:, sl])
        ig = jax.nn.sigmoid(jnp.dot(xb[:, sl], wx_ref[g], preferred_element_type=F32) + bx_ref[:, sl])
        log_a = (-LRU_C * r) * softplus_neg_lam[:, sl]
        a = jnp.exp(log_a)
        th = jnp.tanh(log_a)
        mult = jnp.sqrt(-2.0 * th / (1.0 - th))
        abuf[pad:pad + tt, sl] = a
        ubuf[pad:pad + tt, sl] = mult * (ig * xr[:, sl])

    s = 1
    while s < tt:
        a = abuf[pad:pad + tt, :]
        u = ubuf[pad:pad + tt, :]
        a_sh = abuf[pad - s:pad - s + tt, :]
        u_sh = ubuf[pad - s:pad - s + tt, :]
        ubuf[pad:pad + tt, :] = a * u_sh + u
        abuf[pad:pad + tt, :] = a * a_sh
        s *= 2
    hs = abuf[pad:pad + tt, :] * h_ref[0:1, :] + ubuf[pad:pad + tt, :]
    h_ref[...] = jnp.broadcast_to(hs[tt - 1:tt, :], h_ref.shape)
    yl_ref[...] = _gelu_tanh(lru_ref[:, 0:dl]) * hs


def _conv_lru(conv, lru, conv_w, lru_conv_w, lru_conv_b, w_a, b_a, w_x, b_x, lam, batch, seq, tt=256):
    n = conv.shape[0]
    dc = conv.shape[1] // 3
    dl = lru.shape[1] // 2
    nt = seq // tt
    pad = tt // 2
    row = lambda b, t: (b * nt + t, 0)
    fixed2 = lambda b, t: (0, 0)
    fixed3 = lambda b, t: (0, 0, 0)
    return pl.pallas_call(
        functools.partial(_conv_lru_kernel, tt=tt, dc=dc, dl=dl, pad=pad),
        grid=(batch, nt),
        in_specs=[
            pl.BlockSpec((tt, 3 * dc), row),
            pl.BlockSpec((tt, 2 * dl), row),
            pl.BlockSpec(conv_w.shape, fixed2),
            pl.BlockSpec(lru_conv_w.shape, fixed2),
            pl.BlockSpec((1, dl), fixed2),
            pl.BlockSpec(w_a.shape, fixed3),
            pl.BlockSpec((1, dl), fixed2),
            pl.BlockSpec(w_x.shape, fixed3),
            pl.BlockSpec((1, dl), fixed2),
            pl.BlockSpec((1, dl), fixed2),
        ],
        out_specs=[pl.BlockSpec((tt, dc), row), pl.BlockSpec((tt, dl), row)],
        out_shape=[jax.ShapeDtypeStruct((n, dc), F32), jax.ShapeDtypeStruct((n, dl), F32)],
        scratch_shapes=[
            pltpu.VMEM((tt + SUBLANES, dc), F32),
            pltpu.VMEM((tt + SUBLANES, dl), F32),
            pltpu.VMEM((tt + pad, dl), F32),
            pltpu.VMEM((tt + pad, dl), F32),
            pltpu.VMEM((SUBLANES, dl), F32),
        ],
        compiler_params=_params(2),
        name="conv_lru",
    )(conv, lru, conv_w, lru_conv_w, lru_conv_b.reshape(1, dl), w_a, b_a.reshape(1, dl),
      w_x, b_x.reshape(1, dl), lam.reshape(1, dl))


def _mix_out_kernel(x_ref, yc_ref, ya_ref, yl_ref, g_ref, w_ref, o_ref, y_ref):
    col = 0
    for src in (yc_ref, ya_ref, yl_ref):
        width = src.shape[1]
        y_ref[:, col:col + width] = _rms(src[...], g_ref[:, col:col + width]).astype(BF16)
        col += width
    o_ref[...] = x_ref[...] + jnp.dot(y_ref[...], w_ref[...], preferred_element_type=F32)


def _mix_out(x, yc, ya, yl, gain, w, tm=256):
    n, d = x.shape
    dm = w.shape[0]
    row = lambda i: (i, 0)
    return pl.pallas_call(
        _mix_out_kernel,
        grid=(n // tm,),
        in_specs=[
            pl.BlockSpec((tm, d), row),
            pl.BlockSpec((tm, yc.shape[1]), row),
            pl.BlockSpec((tm, ya.shape[1]), row),
            pl.BlockSpec((tm, yl.shape[1]), row),
            pl.BlockSpec((1, dm), lambda i: (0, 0)),
            pl.BlockSpec(w.shape, lambda i: (0, 0), pipeline_mode=pl.Buffered(1)),
        ],
        out_specs=pl.BlockSpec((tm, d), row),
        out_shape=jax.ShapeDtypeStruct((n, d), F32),
        scratch_shapes=[pltpu.VMEM((tm, dm), BF16)],
        compiler_params=_params(1),
        name="mix_out",
    )(x, yc, ya, yl, gain.reshape(1, dm), w)


def _final_norm_kernel(x_ref, g_ref, o_ref):
    o_ref[...] = _rms(x_ref[...], g_ref[...])


def _final_norm(x, gain, tm=512):
    n, d = x.shape
    return pl.pallas_call(
        _final_norm_kernel,
        grid=(n // tm,),
        in_specs=[pl.BlockSpec((tm, d), lambda i: (i, 0)), pl.BlockSpec((1, d), lambda i: (0, 0))],
        out_specs=pl.BlockSpec((tm, d), lambda i: (i, 0)),
        out_shape=jax.ShapeDtypeStruct((n, d), F32),
        compiler_params=_params(1),
        name="final_norm",
    )(x, gain.reshape(1, d))


def kernel(x, norm_ffn1, ffn1_w_in, ffn1_w_out, norm_mix, mix_w_in, conv_w, fgate_b, lru_conv_w,
           lru_conv_b, lru_w_a, lru_b_a, lru_w_x, lru_b_x, lru_lambda, mix_out_norm, mix_w_out,
           norm_ffn2, ffn2_w_in, ffn2_w_out, final_norm):
    batch, seq, d = x.shape
    depth = norm_ffn1.shape[0]
    d_conv = conv_w.shape[2]
    d_lru = lru_lambda.shape[1]
    d_att = mix_w_out.shape[1] - d_conv - d_lru
    nh = d_att // HEAD_DIM
    assert nh == N_ATT_HEADS == fgate_b.shape[1] and nh <= SUBLANES

    f0 = 3 * d_conv + 3 * d_att
    w_mix = jnp.concatenate(
        [mix_w_in[:, :, :f0], mix_w_in[:, :, f0 + nh:], mix_w_in[:, :, f0:f0 + nh],
         jnp.zeros((depth, d, LANES - nh), mix_w_in.dtype)], axis=2).astype(BF16)
    fbias = jnp.pad(fgate_b, ((0, 0), (0, LANES - nh)))
    w1_in, w1_out = ffn1_w_in.astype(BF16), ffn1_w_out.astype(BF16)
    w2_in, w2_out = ffn2_w_in.astype(BF16), ffn2_w_out.astype(BF16)
    w_mix_out = mix_w_out.astype(BF16)
    w_a, w_x = lru_w_a.astype(BF16), lru_w_x.astype(BF16)

    h = x.reshape(batch * seq, d)
    for l in range(depth):
        h = _ffn(h, norm_ffn1[l], w1_in[l], w1_out[l])
        conv, q, k, v, lru, f = _mix_in(h, norm_mix[l], w_mix[l], d_conv, d_att, d_lru)
        ccol, crow = _fcum(f, fbias[l:l + 1], batch, seq)
        ya = _attention(q, k, v, ccol, crow, batch, seq)
        yc, yl = _conv_lru(conv, lru, conv_w[l], lru_conv_w[l], lru_conv_b[l], w_a[l], lru_b_a[l],
                           w_x[l], lru_b_x[l], lru_lambda[l], batch, seq)
        h = _mix_out(h, yc, ya, yl, mix_out_norm[l], w_mix_out[l])
        h = _ffn(h, norm_ffn2[l], w2_in[l], w2_out[l])
    return _final_norm(h, final_norm).reshape(batch, seq, d)
```

```python
import functools

import jax
import jax.numpy as jnp
from jax import lax
from jax.experimental import pallas as pl
from jax.experimental.pallas import tpu as pltpu

F32 = jnp.float32
BF16 = jnp.bfloat16

EPS = 1e-6
LOG2E = 1.4426950408889634
HEAD_DIM = 128
LRU_BLOCKS = 4
LRU_C = 8.0
CONV_WIDTH = 3
LRU_CONV_WIDTH = 4
LANES = 128
SUBLANES = 8
VMEM_LIMIT = 56 * 1024 * 1024


def _params(n_axes, vmem=VMEM_LIMIT):
    return pltpu.CompilerParams(dimension_semantics=("arbitrary",) * n_axes,
                                vmem_limit_bytes=vmem)


def _rms(x, gain):
    ms = jnp.mean(x * x, axis=-1, keepdims=True)
    return (x * lax.rsqrt(ms + EPS)) * gain


def _ffn_kernel(x_ref, g_ref, wg_ref, wu_ref, wo_ref, o_ref, xn_ref):
    @pl.when(pl.program_id(1) == 0)
    def _():
        x = x_ref[...]
        xn_ref[...] = _rms(x, g_ref[...]).astype(BF16)
        o_ref[...] = x

    xn = xn_ref[...]
    g = jnp.dot(xn, wg_ref[...], preferred_element_type=F32)
    u = jnp.dot(xn, wu_ref[...], preferred_element_type=F32)
    h = (((g * jax.nn.sigmoid(g)) * u) * 0.5).astype(BF16)
    o_ref[...] += jnp.dot(h, wo_ref[...], preferred_element_type=F32)


def _ffn(x, gain, w_in, w_out, layer, tm=1024, tf=512):
    n, d = x.shape
    f = w_out.shape[1]
    nk = f // tf
    return pl.pallas_call(
        _ffn_kernel,
        grid=(n // tm, nk),
        in_specs=[
            pl.BlockSpec((tm, d), lambda i, k: (i, 0)),
            pl.BlockSpec((None, 1, d), lambda i, k: (layer, 0, 0)),
            pl.BlockSpec((None, d, tf), lambda i, k: (layer, 0, k)),
            pl.BlockSpec((None, d, tf), lambda i, k: (layer, 0, k + nk)),
            pl.BlockSpec((None, tf, d), lambda i, k: (layer, k, 0)),
        ],
        out_specs=pl.BlockSpec((tm, d), lambda i, k: (i, 0)),
        out_shape=jax.ShapeDtypeStruct((n, d), F32),
        scratch_shapes=[pltpu.VMEM((tm, d), BF16)],
        compiler_params=_params(2),
        name="ffn",
    )(x, gain.reshape(gain.shape[0], 1, d), w_in, w_in, w_out)


def _mix_in_kernel(x_ref, g_ref, w_ref, conv_ref, q_ref, k_ref, v_ref, lru_ref, f_ref, *, chunk, q_scale):
    xn = _rms(x_ref[...], g_ref[...]).astype(BF16)
    col = 0
    for out in (conv_ref, q_ref, k_ref, v_ref, lru_ref, f_ref):
        width = out.shape[1]
        for c in range(0, width, chunk):
            w = min(chunk, width - c)
            y = jnp.dot(xn, w_ref[:, col + c:col + c + w], preferred_element_type=F32)
            if out is q_ref:
                y = y * q_scale
            out[:, c:c + w] = y.astype(out.dtype)
        col += width


def _mix_in(x, gain, w, layer, d_conv, d_att, d_lru, q_scale, tm=256, chunk=512):
    n, d = x.shape
    widths = (3 * d_conv, d_att, d_att, d_att, 2 * d_lru, LANES)
    dtypes = (F32, BF16, BF16, BF16, F32, F32)
    assert w.shape[1:] == (d, sum(widths))
    return pl.pallas_call(
        functools.partial(_mix_in_kernel, chunk=chunk, q_scale=q_scale),
        grid=(n // tm,),
        in_specs=[
            pl.BlockSpec((tm, d), lambda i: (i, 0)),
            pl.BlockSpec((None, 1, d), lambda i: (layer, 0, 0)),
            pl.BlockSpec((None,) + w.shape[1:], lambda i: (layer, 0, 0), pipeline_mode=pl.Buffered(1)),
        ],
        out_specs=[pl.BlockSpec((tm, wd), lambda i: (i, 0)) for wd in widths],
        out_shape=[jax.ShapeDtypeStruct((n, wd), dt) for wd, dt in zip(widths, dtypes)],
        compiler_params=_params(1),
        name="mix_in",
    )(x, gain.reshape(gain.shape[0], 1, d), w)


def _fcum_kernel(f_ref, b_ref, col_ref, carry_ref):
    @pl.when(pl.program_id(1) == 0)
    def _():
        carry_ref[...] = jnp.zeros_like(carry_ref)

    z = f_ref[...] + b_ref[...]
    log_f = -(jnp.maximum(-z, 0.0) + jnp.log1p(jnp.exp(-jnp.abs(z))))
    t = log_f.shape[0]
    r = lax.broadcasted_iota(jnp.int32, (t, t), 0)
    c = lax.broadcasted_iota(jnp.int32, (t, t), 1)
    tri = (c <= r).astype(F32)
    cum = jnp.dot(tri, log_f, preferred_element_type=F32,
                  precision=lax.Precision.HIGHEST) + carry_ref[0:1, :]
    col_ref[...] = cum
    carry_ref[...] = jnp.broadcast_to(cum[t - 1:t, :], carry_ref.shape)


def _fcum(f, bias, batch, seq, tc=256):
    n = f.shape[0]
    nt = seq // tc
    return pl.pallas_call(
        _fcum_kernel,
        grid=(batch, nt),
        in_specs=[
            pl.BlockSpec((tc, LANES), lambda b, t: (b * nt + t, 0)),
            pl.BlockSpec((1, LANES), lambda b, t: (0, 0)),
        ],
        out_specs=pl.BlockSpec((tc, LANES), lambda b, t: (b * nt + t, 0)),
        out_shape=jax.ShapeDtypeStruct((n, LANES), F32),
        scratch_shapes=[pltpu.VMEM((SUBLANES, LANES), F32)],
        compiler_params=_params(2),
        name="fcum",
    )(f, bias)


def _decay_columns(c_ref_block, head, on_query_side):
    lane = lax.broadcasted_iota(jnp.int32, c_ref_block.shape, 1)
    c = jnp.sum(jnp.where(lane == head, c_ref_block, 0.0), axis=1, keepdims=True) * LOG2E
    hi = c.astype(BF16).astype(F32)
    rest = c - hi
    mid = rest.astype(BF16).astype(F32)
    lo = rest - mid
    if on_query_side:
        parts, base = (hi, mid, lo), 0
    else:
        parts, base = (-hi, -mid, -lo), 3
    ones_at = 3 - base
    out = jnp.where((lane >= ones_at) & (lane < ones_at + 3), 1.0, 0.0)
    for idx, part in enumerate(parts):
        out = jnp.where(lane == base + idx, part, out)
    return out.astype(BF16)


def _attn_kernel(q_ref, k_ref, v_ref, cq_ref, ck_ref, o_ref, kx_ref, *, tq, build_rows):
    i = pl.program_id(2)
    dh = HEAD_DIM
    hp = q_ref.shape[1] // dh
    heads = [(pl.program_id(1) * hp + g, slice(g * dh, (g + 1) * dh)) for g in range(hp)]

    @pl.when(i == 0)
    def _():
        def build(c, carry):
            rows = pl.ds(pl.multiple_of(c * build_rows, build_rows), build_rows)
            for g, (head, cols) in enumerate(heads):
                kx_ref[g, rows, 0:dh] = k_ref[rows, cols]
                kx_ref[g, rows, dh:2 * dh] = _decay_columns(ck_ref[rows, :], head, on_query_side=False)
            return carry
        lax.fori_loop(0, k_ref.shape[0] // build_rows, build, 0)

    qx_t = []
    for head, cols in heads:
        qx = jnp.concatenate([q_ref[:, cols].astype(F32),
                              _decay_columns(cq_ref[...], head, on_query_side=True).astype(F32)], axis=1)
        qx_t.append(qx.T.astype(BF16))

    def update(g, j, carry, masked):
        m, l, acc = carry
        rows = pl.ds(pl.multiple_of(j * tq, tq), tq)
        s = jnp.dot(kx_ref[g, rows, :], qx_t[g], preferred_element_type=F32)
        if masked:
            r = lax.broadcasted_iota(jnp.int32, s.shape, 0)
            c = lax.broadcasted_iota(jnp.int32, s.shape, 1)
            s = jnp.where(r <= c, s, -jnp.inf)
        m_new = jnp.maximum(m, jnp.max(s, axis=0, keepdims=True))
        alpha = jnp.exp2(m - m_new)
        p = jnp.exp2(s - m_new)
        l = alpha * l + jnp.sum(p, axis=0, keepdims=True)
        pv = lax.dot_general(v_ref[rows, heads[g][1]], p.astype(BF16), (((0,), (0,)), ((), ())),
                             preferred_element_type=F32)
        return m_new, l, alpha * acc + pv

    def body(j, carries):
        return tuple(update(g, j, carries[g], masked=False) for g in range(hp))

    init = (jnp.full((1, tq), -jnp.inf, F32), jnp.zeros((1, tq), F32), jnp.zeros((dh, tq), F32))
    carries = lax.fori_loop(0, i, body, (init,) * hp)
    for g, (_, cols) in enumerate(heads):
        m, l, acc = update(g, i, carries[g], masked=True)
        o_ref[:, cols] = (acc * (1.0 / l)).T


def _attention(q, k, v, cum, batch, seq, tq=512, hp=2, build_rows=512):
    n, d_att = q.shape
    nh = d_att // HEAD_DIM
    nq = seq // tq
    width = hp * HEAD_DIM
    return pl.pallas_call(
        functools.partial(_attn_kernel, tq=tq, build_rows=build_rows),
        grid=(batch, nh // hp, nq),
        in_specs=[
            pl.BlockSpec((tq, width), lambda b, h, i: (b * nq + i, h)),
            pl.BlockSpec((seq, width), lambda b, h, i: (b, h)),
            pl.BlockSpec((seq, width), lambda b, h, i: (b, h)),
            pl.BlockSpec((tq, LANES), lambda b, h, i: (b * nq + i, 0)),
            pl.BlockSpec((seq, LANES), lambda b, h, i: (b, 0)),
        ],
        out_specs=pl.BlockSpec((tq, width), lambda b, h, i: (b * nq + i, h)),
        out_shape=jax.ShapeDtypeStruct((n, d_att), F32),
        scratch_shapes=[pltpu.VMEM((hp, seq, 2 * HEAD_DIM), BF16)],
        compiler_params=_params(3),
        name="fox_attention",
    )(q, k, v, cum, cum)


def _gelu_tanh(x):
    return 0.5 * x * (1.0 + jnp.tanh(0.7978845608028654 * (x + 0.044715 * (x * x * x))))


def _conv_lru_kernel(conv_ref, lru_ref, cw_ref, lw_ref, lb_ref, wa_ref, ba_ref, wx_ref, bx_ref,
                     lam_ref, yc_ref, yl_ref, zbuf, xbuf, abuf, ubuf, h_ref, *, tt, dc, dl, pad):
    @pl.when(pl.program_id(1) == 0)
    def _():
        zbuf[0:SUBLANES, :] = jnp.zeros((SUBLANES, dc), F32)
        xbuf[0:SUBLANES, :] = jnp.zeros((SUBLANES, dl), F32)
        h_ref[...] = jnp.zeros_like(h_ref)
        abuf[0:pad, :] = jnp.ones((pad, dl), F32)
        ubuf[0:pad, :] = jnp.zeros((pad, dl), F32)

    o = SUBLANES
    zbuf[o:o + tt, :] = conv_ref[:, dc:2 * dc] * conv_ref[:, 2 * dc:3 * dc]
    yc = cw_ref[CONV_WIDTH - 1:CONV_WIDTH, :] * zbuf[o:o + tt, :]
    for s in range(1, CONV_WIDTH):
        yc = yc + cw_ref[CONV_WIDTH - 1 - s:CONV_WIDTH - s, :] * zbuf[o - s:o - s + tt, :]
    yc_ref[...] = conv_ref[:, 0:dc] * yc
    zbuf[0:o, :] = zbuf[tt:tt + o, :]

    xbuf[o:o + tt, :] = lru_ref[:, dl:2 * dl]
    xr = lw_ref[LRU_CONV_WIDTH - 1:LRU_CONV_WIDTH, :] * xbuf[o:o + tt, :]
    for s in range(1, LRU_CONV_WIDTH):
        xr = xr + lw_ref[LRU_CONV_WIDTH - 1 - s:LRU_CONV_WIDTH - s, :] * xbuf[o - s:o - s + tt, :]
    xr = xr + lb_ref[...]
    xbuf[0:o, :] = xbuf[tt:tt + o, :]

    xb = xr.astype(BF16)
    blk = dl // LRU_BLOCKS
    lam = lam_ref[...]
    softplus_neg_lam = jnp.maximum(-lam, 0.0) + jnp.log1p(jnp.exp(-jnp.abs(lam)))
    for g in range(LRU_BLOCKS):
        sl = slice(g * blk, (g + 1) * blk)
        r = jax.nn.sigmoid(jnp.dot(xb[:, sl], wa_ref[g], preferred_element_type=F32) + ba_ref[:, sl])
        ig = jax.nn.sigmoid(jnp.dot(xb[:, sl], wx_ref[g], preferred_element_type=F32) + bx_ref[:, sl])
        log_a = (-LRU_C * r) * softplus_neg_lam[:, sl]
        a = jnp.exp(log_a)
        th = jnp.tanh(log_a)
        mult = jnp.sqrt(-2.0 * th / (1.0 - th))
        abuf[pad:pad + tt, sl] = a
        ubuf[pad:pad + tt, sl] = mult * (ig * xr[:, sl])

    s = 1
    while s < tt:
        a = abuf[pad:pad + tt, :]
        u = ubuf[pad:pad + tt, :]
        a_sh = abuf[pad - s:pad - s + tt, :]
        u_sh = ubuf[pad - s:pad - s + tt, :]
        ubuf[pad:pad + tt, :] = a * u_sh + u
        abuf[pad:pad + tt, :] = a * a_sh
        s *= 2
    hs = abuf[pad:pad + tt, :] * h_ref[0:1, :] + ubuf[pad:pad + tt, :]
    h_ref[...] = jnp.broadcast_to(hs[tt - 1:tt, :], h_ref.shape)
    yl_ref[...] = _gelu_tanh(lru_ref[:, 0:dl]) * hs


def _conv_lru(conv, lru, conv_w, lru_conv_w, lru_conv_b, w_a, b_a, w_x, b_x, lam, batch, seq, tt=256):
    n = conv.shape[0]
    dc = conv.shape[1] // 3
    dl = lru.shape[1] // 2
    nt = seq // tt
    pad = tt // 2
    row = lambda b, t: (b * nt + t, 0)
    fixed2 = lambda b, t: (0, 0)
    fixed3 = lambda b, t: (0, 0, 0)
    return pl.pallas_call(
        functools.partial(_conv_lru_kernel, tt=tt, dc=dc, dl=dl, pad=pad),
        grid=(batch, nt),
        in_specs=[
            pl.BlockSpec((tt, 3 * dc), row),
            pl.BlockSpec((tt, 2 * dl), row),
            pl.BlockSpec(conv_w.shape, fixed2),
            pl.BlockSpec(lru_conv_w.shape, fixed2),
            pl.BlockSpec((1, dl), fixed2),
            pl.BlockSpec(w_a.shape, fixed3),
            pl.BlockSpec((1, dl), fixed2),
            pl.BlockSpec(w_x.shape, fixed3),
            pl.BlockSpec((1, dl), fixed2),
            pl.BlockSpec((1, dl), fixed2),
        ],
        out_specs=[pl.BlockSpec((tt, dc), row), pl.BlockSpec((tt, dl), row)],
        out_shape=[jax.ShapeDtypeStruct((n, dc), F32), jax.ShapeDtypeStruct((n, dl), F32)],
        scratch_shapes=[
            pltpu.VMEM((tt + SUBLANES, dc), F32),
            pltpu.VMEM((tt + SUBLANES, dl), F32),
            pltpu.VMEM((tt + pad, dl), F32),
            pltpu.VMEM((tt + pad, dl), F32),
            pltpu.VMEM((SUBLANES, dl), F32),
        ],
        compiler_params=_params(2),
        name="conv_lru",
    )(conv, lru, conv_w, lru_conv_w, lru_conv_b.reshape(1, dl), w_a, b_a.reshape(1, dl),
      w_x, b_x.reshape(1, dl), lam.reshape(1, dl))


def _mix_out_kernel(x_ref, yc_ref, ya_ref, yl_ref, g_ref, w_ref, o_ref, y_ref):
    col = 0
    for src in (yc_ref, ya_ref, yl_ref):
        width = src.shape[1]
        y_ref[:, col:col + width] = _rms(src[...], g_ref[:, col:col + width]).astype(BF16)
        col += width
    o_ref[...] = x_ref[...] + jnp.dot(y_ref[...], w_ref[...], preferred_element_type=F32)


def _mix_out(x, yc, ya, yl, gain, w, layer, tm=512):
    n, d = x.shape
    dm = w.shape[1]
    row = lambda i: (i, 0)
    return pl.pallas_call(
        _mix_out_kernel,
        grid=(n // tm,),
        in_specs=[
            pl.BlockSpec((tm, d), row),
            pl.BlockSpec((tm, yc.shape[1]), row),
            pl.BlockSpec((tm, ya.shape[1]), row),
            pl.BlockSpec((tm, yl.shape[1]), row),
            pl.BlockSpec((None, 1, dm), lambda i: (layer, 0, 0)),
            pl.BlockSpec((None,) + w.shape[1:], lambda i: (layer, 0, 0), pipeline_mode=pl.Buffered(1)),
        ],
        out_specs=pl.BlockSpec((tm, d), row),
        out_shape=jax.ShapeDtypeStruct((n, d), F32),
        scratch_shapes=[pltpu.VMEM((tm, dm), BF16)],
        compiler_params=_params(1),
        name="mix_out",
    )(x, yc, ya, yl, gain.reshape(gain.shape[0], 1, dm), w)


def _final_norm_kernel(x_ref, g_ref, o_ref):
    o_ref[...] = _rms(x_ref[...], g_ref[...])


def _final_norm(x, gain, tm=512):
    n, d = x.shape
    return pl.pallas_call(
        _final_norm_kernel,
        grid=(n // tm,),
        in_specs=[pl.BlockSpec((tm, d), lambda i: (i, 0)), pl.BlockSpec((1, d), lambda i: (0, 0))],
        out_specs=pl.BlockSpec((tm, d), lambda i: (i, 0)),
        out_shape=jax.ShapeDtypeStruct((n, d), F32),
        compiler_params=_params(1),
        name="final_norm",
    )(x, gain.reshape(1, d))


def kernel(x, norm_ffn1, ffn1_w_in, ffn1_w_out, norm_mix, mix_w_in, conv_w, fgate_b, lru_conv_w,
           lru_conv_b, lru_w_a, lru_b_a, lru_w_x, lru_b_x, lru_lambda, mix_out_norm, mix_w_out,
           norm_ffn2, ffn2_w_in, ffn2_w_out, final_norm):
    batch, seq, d = x.shape
    depth = norm_ffn1.shape[0]
    d_conv = conv_w.shape[2]
    d_lru = lru_lambda.shape[1]
    d_att = mix_w_out.shape[1] - d_conv - d_lru
    nh = d_att // HEAD_DIM
    assert nh == fgate_b.shape[1] and nh + 6 <= LANES

    f0 = 3 * d_conv + 3 * d_att
    w_mix = jnp.concatenate(
        [mix_w_in[:, :, :f0].astype(BF16), mix_w_in[:, :, f0 + nh:].astype(BF16),
         mix_w_in[:, :, f0:f0 + nh].astype(BF16), jnp.zeros((depth, d, LANES - nh), BF16)], axis=2)
    fbias = jnp.pad(fgate_b, ((0, 0), (0, LANES - nh)))
    w1_in, w1_out = ffn1_w_in.astype(BF16), ffn1_w_out.astype(BF16)
    w2_in, w2_out = ffn2_w_in.astype(BF16), ffn2_w_out.astype(BF16)
    w_mix_out = mix_w_out.astype(BF16)
    w_a, w_x = lru_w_a.astype(BF16), lru_w_x.astype(BF16)
    q_scale = LOG2E * HEAD_DIM ** -0.5

    h = x.reshape(batch * seq, d)
    for l in range(depth):
        h = _ffn(h, norm_ffn1, w1_in, w1_out, l)
        conv, q, k, v, lru, f = _mix_in(h, norm_mix, w_mix, l, d_conv, d_att, d_lru, q_scale)
        cum = _fcum(f, fbias[l:l + 1], batch, seq)
        ya = _attention(q, k, v, cum, batch, seq)
        yc, yl = _conv_lru(conv, lru, conv_w[l], lru_conv_w[l], lru_conv_b[l], w_a[l], lru_b_a[l],
                           w_x[l], lru_b_x[l], lru_lambda[l], batch, seq)
        h = _mix_out(h, yc, ya, yl, mix_out_norm, w_mix_out, l)
        h = _ffn(h, norm_ffn2, w2_in, w2_out, l)
    return _final_norm(h, final_norm).reshape(batch, seq, d)
```

```python
import functools

import jax
import jax.numpy as jnp
from jax import lax
from jax.experimental import pallas as pl
from jax.experimental.pallas import tpu as pltpu

F32 = jnp.float32
BF16 = jnp.bfloat16

EPS = 1e-6
LOG2E = 1.4426950408889634
HEAD_DIM = 128
LRU_BLOCKS = 4
LRU_C = 8.0
CONV_WIDTH = 3
LRU_CONV_WIDTH = 4
LANES = 128
SUBLANES = 8
VMEM_LIMIT = 56 * 1024 * 1024


def _params(n_axes, vmem=VMEM_LIMIT):
    return pltpu.CompilerParams(dimension_semantics=("arbitrary",) * n_axes,
                                vmem_limit_bytes=vmem)


def _rms(x, gain):
    ms = jnp.mean(x * x, axis=-1, keepdims=True)
    return (x * lax.rsqrt(ms + EPS)) * gain


def _ffn_kernel(x_ref, g_ref, wg_ref, wu_ref, wo_ref, *rest, first_step_chunks):
    (gf_ref, o_ref, xn_ref) = rest if len(rest) == 3 else (None,) + rest
    k = pl.program_id(1)

    def swiglu_half(xn):
        g = jnp.dot(xn, wg_ref[...], preferred_element_type=F32)
        u = jnp.dot(xn, wu_ref[...], preferred_element_type=F32)
        h = (((g * jax.nn.sigmoid(g)) * u) * 0.5).astype(BF16)
        return jnp.dot(h, wo_ref[...], preferred_element_type=F32)

    @pl.when(k == 0)
    def _():
        rc = x_ref.shape[0] // first_step_chunks
        for c in range(first_step_chunks):
            rows = slice(c * rc, (c + 1) * rc)
            x = x_ref[rows, :]
            xn = _rms(x, g_ref[...]).astype(BF16)
            xn_ref[rows, :] = xn
            o_ref[rows, :] = x + swiglu_half(xn)

    @pl.when(k != 0)
    def _():
        o_ref[...] += swiglu_half(xn_ref[...])

    if gf_ref is not None:
        @pl.when(k == pl.num_programs(1) - 1)
        def _():
            o_ref[...] = _rms(o_ref[...], gf_ref[...])


def _ffn(x, gain, w_in, w_out, layer, final_gain=None, tm=1024, tf=512, first_step_chunks=4):
    n, d = x.shape
    f = w_out.shape[1]
    nk = f // tf
    in_specs = [
        pl.BlockSpec((tm, d), lambda i, k: (i, 0)),
        pl.BlockSpec((None, 1, d), lambda i, k: (layer, 0, 0)),
        pl.BlockSpec((None, d, tf), lambda i, k: (layer, 0, k)),
        pl.BlockSpec((None, d, tf), lambda i, k: (layer, 0, k + nk)),
        pl.BlockSpec((None, tf, d), lambda i, k: (layer, k, 0)),
    ]
    args = [x, gain.reshape(gain.shape[0], 1, d), w_in, w_in, w_out]
    if final_gain is not None:
        in_specs.append(pl.BlockSpec((1, d), lambda i, k: (0, 0)))
        args.append(final_gain.reshape(1, d))
    return pl.pallas_call(
        functools.partial(_ffn_kernel, first_step_chunks=first_step_chunks),
        grid=(n // tm, nk),
        in_specs=in_specs,
        out_specs=pl.BlockSpec((tm, d), lambda i, k: (i, 0)),
        out_shape=jax.ShapeDtypeStruct((n, d), F32),
        scratch_shapes=[pltpu.VMEM((tm, d), BF16)],
        compiler_params=_params(2),
        name="ffn",
    )(*args)


def _mix_in_kernel(x_ref, g_ref, wm_ref, wl_ref, wf_ref, conv_ref, q_ref, k_ref, v_ref, lru_ref, f_ref,
                   *, chunk, q_scale):
    xn = _rms(x_ref[...], g_ref[...]).astype(BF16)

    def project(w_ref, col, out):
        for c in range(0, out.shape[1], chunk):
            w = min(chunk, out.shape[1] - c)
            y = jnp.dot(xn, w_ref[:, col + c:col + c + w], preferred_element_type=F32)
            if out is q_ref:
                y = y * q_scale
            out[:, c:c + w] = y.astype(out.dtype)
        return col + out.shape[1]

    col = 0
    for out in (conv_ref, q_ref, k_ref, v_ref):
        col = project(wm_ref, col, out)
    project(wl_ref, 0, lru_ref)
    project(wf_ref, 0, f_ref)


def _mix_in(x, gain, w_main, w_lru, w_f, layer, d_conv, d_att, q_scale, tm=256, chunk=512):
    n, d = x.shape
    widths = (3 * d_conv, d_att, d_att, d_att, w_lru.shape[2], w_f.shape[2])
    dtypes = (F32, BF16, BF16, BF16, F32, F32)
    assert w_main.shape[2] == sum(widths[:4])
    resident = lambda w: pl.BlockSpec((None,) + w.shape[1:], lambda i: (layer, 0, 0),
                                      pipeline_mode=pl.Buffered(1))
    return pl.pallas_call(
        functools.partial(_mix_in_kernel, chunk=chunk, q_scale=q_scale),
        grid=(n // tm,),
        in_specs=[
            pl.BlockSpec((tm, d), lambda i: (i, 0)),
            pl.BlockSpec((None, 1, d), lambda i: (layer, 0, 0)),
            resident(w_main), resident(w_lru), resident(w_f),
        ],
        out_specs=[pl.BlockSpec((tm, wd), lambda i: (i, 0)) for wd in widths],
        out_shape=[jax.ShapeDtypeStruct((n, wd), dt) for wd, dt in zip(widths, dtypes)],
        compiler_params=_params(1),
        name="mix_in",
    )(x, gain.reshape(gain.shape[0], 1, d), w_main, w_lru, w_f)


def _fcum_kernel(f_ref, b_ref, col_ref, carry_ref):
    @pl.when(pl.program_id(1) == 0)
    def _():
        carry_ref[...] = jnp.zeros_like(carry_ref)

    z = f_ref[...] + b_ref[...]
    log_f = -(jnp.maximum(-z, 0.0) + jnp.log1p(jnp.exp(-jnp.abs(z))))
    t = log_f.shape[0]
    r = lax.broadcasted_iota(jnp.int32, (t, t), 0)
    c = lax.broadcasted_iota(jnp.int32, (t, t), 1)
    tri = (c <= r).astype(F32)
    cum = jnp.dot(tri, log_f, preferred_element_type=F32,
                  precision=lax.Precision.HIGHEST) + carry_ref[0:1, :]
    col_ref[...] = cum
    carry_ref[...] = jnp.broadcast_to(cum[t - 1:t, :], carry_ref.shape)


def _fcum(f, bias, batch, seq, tc=256):
    n = f.shape[0]
    nt = seq // tc
    return pl.pallas_call(
        _fcum_kernel,
        grid=(batch, nt),
        in_specs=[
            pl.BlockSpec((tc, LANES), lambda b, t: (b * nt + t, 0)),
            pl.BlockSpec((1, LANES), lambda b, t: (0, 0)),
        ],
        out_specs=pl.BlockSpec((tc, LANES), lambda b, t: (b * nt + t, 0)),
        out_shape=jax.ShapeDtypeStruct((n, LANES), F32),
        scratch_shapes=[pltpu.VMEM((SUBLANES, LANES), F32)],
        compiler_params=_params(2),
        name="fcum",
    )(f, bias)


DECAY_COPIES = 6


def _decay_split(c_block):
    c = c_block * LOG2E
    hi = c.astype(BF16).astype(F32)
    rest = c - hi
    mid = rest.astype(BF16).astype(F32)
    return hi, mid, rest - mid


def _decay_columns(pieces, head, nh, on_query_side):
    hi, mid, lo = pieces
    lane = lax.broadcasted_iota(jnp.int32, hi.shape, 1)
    onehot = jnp.where(lane % nh == head, 1.0, 0.0)
    first = 0 if on_query_side else 3 * nh
    split = jnp.where(lane < first + nh, hi, jnp.where(lane < first + 2 * nh, mid, lo))
    if on_query_side:
        out = jnp.where(lane < 3 * nh, split, jnp.where(lane < 6 * nh, onehot, 0.0))
    else:
        out = jnp.where(lane < 3 * nh, onehot, jnp.where(lane < 6 * nh, -split, 0.0))
    return out.astype(BF16)


def _attn_kernel(q_ref, k_ref, v_ref, cq_ref, ck_ref, o_ref, kx_ref, *, tq, nh, build_rows):
    i = pl.program_id(2)
    dh = HEAD_DIM
    hp = q_ref.shape[1] // dh
    heads = [(pl.program_id(1) * hp + g, slice(g * dh, (g + 1) * dh)) for g in range(hp)]

    @pl.when(i == 0)
    def _():
        def build(c, carry):
            rows = pl.ds(pl.multiple_of(c * build_rows, build_rows), build_rows)
            pieces = _decay_split(ck_ref[rows, :])
            for g, (head, cols) in enumerate(heads):
                kx_ref[g, rows, 0:dh] = k_ref[rows, cols]
                kx_ref[g, rows, dh:2 * dh] = _decay_columns(pieces, head, nh, on_query_side=False)
            return carry
        lax.fori_loop(0, k_ref.shape[0] // build_rows, build, 0)

    pieces = _decay_split(cq_ref[...])
    qx_t = []
    for head, cols in heads:
        qx = jnp.concatenate([q_ref[:, cols].astype(F32),
                              _decay_columns(pieces, head, nh, on_query_side=True).astype(F32)], axis=1)
        qx_t.append(qx.T.astype(BF16))

    def update(j, carries, masked):
        rows = pl.ds(pl.multiple_of(j * tq, tq), tq)
        scores = [jnp.dot(kx_ref[g, rows, :], qx_t[g], preferred_element_type=F32)
                  for g in range(hp)]
        probs, new = [], []
        for g, s in enumerate(scores):
            m, l, acc = carries[g]
            if masked:
                r = lax.broadcasted_iota(jnp.int32, s.shape, 0)
                c = lax.broadcasted_iota(jnp.int32, s.shape, 1)
                s = jnp.where(r <= c, s, -jnp.inf)
            m_new = jnp.maximum(m, jnp.max(s, axis=0, keepdims=True))
            alpha = jnp.exp2(m - m_new)
            p = jnp.exp2(s - m_new)
            probs.append(p.astype(BF16))
            new.append((m_new, alpha * l + jnp.sum(p, axis=0, keepdims=True), alpha * acc))
        out = []
        for g, p in enumerate(probs):
            pv = lax.dot_general(v_ref[rows, heads[g][1]], p, (((0,), (0,)), ((), ())),
                                 preferred_element_type=F32)
            out.append((new[g][0], new[g][1], new[g][2] + pv))
        return tuple(out)

    init = (jnp.full((1, tq), -jnp.inf, F32), jnp.zeros((1, tq), F32), jnp.zeros((dh, tq), F32))
    carries = lax.fori_loop(0, i, functools.partial(update, masked=False), (init,) * hp)
    carries = update(i, carries, masked=True)
    for g, (_, cols) in enumerate(heads):
        m, l, acc = carries[g]
        o_ref[:, cols] = (acc * (1.0 / l)).T


def _attention(q, k, v, cum, batch, seq, tq=512, hp=4, build_rows=512):
    n, d_att = q.shape
    nh = d_att // HEAD_DIM
    nq = seq // tq
    width = hp * HEAD_DIM
    return pl.pallas_call(
        functools.partial(_attn_kernel, tq=tq, nh=nh, build_rows=build_rows),
        grid=(batch, nh // hp, nq),
        in_specs=[
            pl.BlockSpec((tq, width), lambda b, h, i: (b * nq + i, h)),
            pl.BlockSpec((seq, width), lambda b, h, i: (b, h)),
            pl.BlockSpec((seq, width), lambda b, h, i: (b, h)),
            pl.BlockSpec((tq, LANES), lambda b, h, i: (b * nq + i, 0)),
            pl.BlockSpec((seq, LANES), lambda b, h, i: (b, 0)),
        ],
        out_specs=pl.BlockSpec((tq, width), lambda b, h, i: (b * nq + i, h)),
        out_shape=jax.ShapeDtypeStruct((n, d_att), F32),
        scratch_shapes=[pltpu.VMEM((hp, seq, 2 * HEAD_DIM), BF16)],
        compiler_params=_params(3),
        name="fox_attention",
    )(q, k, v, cum, cum)


def _gelu_tanh(x):
    return 0.5 * x * (1.0 + jnp.tanh(0.7978845608028654 * (x + 0.044715 * (x * x * x))))


def _conv_lru_kernel(conv_ref, lru_ref, cw_ref, lw_ref, lb_ref, wa_ref, ba_ref, wx_ref, bx_ref,
                     lam_ref, yc_ref, yl_ref, zbuf, xbuf, abuf, ubuf, h_ref, *, tt, dc, dl, pad):
    @pl.when(pl.program_id(1) == 0)
    def _():
        zbuf[0:SUBLANES, :] = jnp.zeros((SUBLANES, dc), F32)
        xbuf[0:SUBLANES, :] = jnp.zeros((SUBLANES, dl), F32)
        h_ref[...] = jnp.zeros_like(h_ref)
        abuf[0:pad, :] = jnp.ones((pad, dl), F32)
        ubuf[0:pad, :] = jnp.zeros((pad, dl), F32)

    o = SUBLANES
    zbuf[o:o + tt, :] = conv_ref[:, dc:2 * dc] * conv_ref[:, 2 * dc:3 * dc]
    yc = cw_ref[CONV_WIDTH - 1:CONV_WIDTH, :] * zbuf[o:o + tt, :]
    for s in range(1, CONV_WIDTH):
        yc = yc + cw_ref[CONV_WIDTH - 1 - s:CONV_WIDTH - s, :] * zbuf[o - s:o - s + tt, :]
    yc_ref[...] = conv_ref[:, 0:dc] * yc
    zbuf[0:o, :] = zbuf[tt:tt + o, :]

    xbuf[o:o + tt, :] = lru_ref[:, dl:2 * dl]
    xr = lw_ref[LRU_CONV_WIDTH - 1:LRU_CONV_WIDTH, :] * xbuf[o:o + tt, :]
    for s in range(1, LRU_CONV_WIDTH):
        xr = xr + lw_ref[LRU_CONV_WIDTH - 1 - s:LRU_CONV_WIDTH - s, :] * xbuf[o - s:o - s + tt, :]
    xr = xr + lb_ref[...]
    xbuf[0:o, :] = xbuf[tt:tt + o, :]

    xb = xr.astype(BF16)
    blk = dl // LRU_BLOCKS
    lam = lam_ref[...]
    softplus_neg_lam = jnp.maximum(-lam, 0.0) + jnp.log1p(jnp.exp(-jnp.abs(lam)))
    for g in range(LRU_BLOCKS):
        sl = slice(g * blk, (g + 1) * blk)
        r = jax.nn.sigmoid(jnp.dot(xb[:, sl], wa_ref[g], preferred_element_type=F32) + ba_ref[:, sl])
        ig = jax.nn.sigmoid(jnp.dot(xb[:, sl], wx_ref[g], preferred_element_type=F32) + bx_ref[:, sl])
        log_a = (-LRU_C * r) * softplus_neg_lam[:, sl]
        a = jnp.exp(log_a)
        th = jnp.tanh(log_a)
        mult = jnp.sqrt(-2.0 * th / (1.0 - th))
        abuf[pad:pad + tt, sl] = a
        ubuf[pad:pad + tt, sl] = mult * (ig * xr[:, sl])

    s = 1
    while s < tt:
        a = abuf[pad:pad + tt, :]
        u = ubuf[pad:pad + tt, :]
        a_sh = abuf[pad - s:pad - s + tt, :]
        u_sh = ubuf[pad - s:pad - s + tt, :]
        ubuf[pad:pad + tt, :] = a * u_sh + u
        abuf[pad:pad + tt, :] = a * a_sh
        s *= 2
    hs = abuf[pad:pad + tt, :] * h_ref[0:1, :] + ubuf[pad:pad + tt, :]
    h_ref[...] = jnp.broadcast_to(hs[tt - 1:tt, :], h_ref.shape)
    yl_ref[...] = _gelu_tanh(lru_ref[:, 0:dl]) * hs


def _conv_lru(conv, lru, conv_w, lru_conv_w, lru_conv_b, w_a, b_a, w_x, b_x, lam, batch, seq, tt=256):
    n = conv.shape[0]
    dc = conv.shape[1] // 3
    dl = lru.shape[1] // 2
    nt = seq // tt
    pad = tt // 2
    row = lambda b, t: (b * nt + t, 0)
    fixed2 = lambda b, t: (0, 0)
    fixed3 = lambda b, t: (0, 0, 0)
    return pl.pallas_call(
        functools.partial(_conv_lru_kernel, tt=tt, dc=dc, dl=dl, pad=pad),
        grid=(batch, nt),
        in_specs=[
            pl.BlockSpec((tt, 3 * dc), row),
            pl.BlockSpec((tt, 2 * dl), row),
            pl.BlockSpec(conv_w.shape, fixed2),
            pl.BlockSpec(lru_conv_w.shape, fixed2),
            pl.BlockSpec((1, dl), fixed2),
            pl.BlockSpec(w_a.shape, fixed3),
            pl.BlockSpec((1, dl), fixed2),
            pl.BlockSpec(w_x.shape, fixed3),
            pl.BlockSpec((1, dl), fixed2),
            pl.BlockSpec((1, dl), fixed2),
        ],
        out_specs=[pl.BlockSpec((tt, dc), row), pl.BlockSpec((tt, dl), row)],
        out_shape=[jax.ShapeDtypeStruct((n, dc), F32), jax.ShapeDtypeStruct((n, dl), F32)],
        scratch_shapes=[
            pltpu.VMEM((tt + SUBLANES, dc), F32),
            pltpu.VMEM((tt + SUBLANES, dl), F32),
            pltpu.VMEM((tt + pad, dl), F32),
            pltpu.VMEM((tt + pad, dl), F32),
            pltpu.VMEM((SUBLANES, dl), F32),
        ],
        compiler_params=_params(2),
        name="conv_lru",
    )(conv, lru, conv_w, lru_conv_w, lru_conv_b.reshape(1, dl), w_a, b_a.reshape(1, dl),
      w_x, b_x.reshape(1, dl), lam.reshape(1, dl))


def _mix_out_kernel(x_ref, yc_ref, ya_ref, yl_ref, g_ref, w_ref, o_ref, y_ref):
    col = 0
    for src in (yc_ref, ya_ref, yl_ref):
        width = src.shape[1]
        y_ref[:, col:col + width] = _rms(src[...], g_ref[:, col:col + width]).astype(BF16)
        col += width
    o_ref[...] = x_ref[...] + jnp.dot(y_ref[...], w_ref[...], preferred_element_type=F32)


def _mix_out(x, yc, ya, yl, gain, w, layer, tm=512):
    n, d = x.shape
    dm = w.shape[1]
    row = lambda i: (i, 0)
    return pl.pallas_call(
        _mix_out_kernel,
        grid=(n // tm,),
        in_specs=[
            pl.BlockSpec((tm, d), row),
            pl.BlockSpec((tm, yc.shape[1]), row),
            pl.BlockSpec((tm, ya.shape[1]), row),
            pl.BlockSpec((tm, yl.shape[1]), row),
            pl.BlockSpec((None, 1, dm), lambda i: (layer, 0, 0)),
            pl.BlockSpec((None,) + w.shape[1:], lambda i: (layer, 0, 0), pipeline_mode=pl.Buffered(1)),
        ],
        out_specs=pl.BlockSpec((tm, d), row),
        out_shape=jax.ShapeDtypeStruct((n, d), F32),
        scratch_shapes=[pltpu.VMEM((tm, dm), BF16)],
        compiler_params=_params(1),
        name="mix_out",
    )(x, yc, ya, yl, gain.reshape(gain.shape[0], 1, dm), w)


def kernel(x, norm_ffn1, ffn1_w_in, ffn1_w_out, norm_mix, mix_w_in, conv_w, fgate_b, lru_conv_w,
           lru_conv_b, lru_w_a, lru_b_a, lru_w_x, lru_b_x, lru_lambda, mix_out_norm, mix_w_out,
           norm_ffn2, ffn2_w_in, ffn2_w_out, final_norm):
    batch, seq, d = x.shape
    depth = norm_ffn1.shape[0]
    d_conv = conv_w.shape[2]
    d_lru = lru_lambda.shape[1]
    d_att = mix_w_out.shape[1] - d_conv - d_lru
    nh = d_att // HEAD_DIM
    assert nh == fgate_b.shape[1] and DECAY_COPIES * nh <= LANES

    f0 = 3 * d_conv + 3 * d_att
    f_pad = LANES - DECAY_COPIES * nh
    w_mix_main = mix_w_in[:, :, :f0].astype(BF16)
    w_mix_lru = mix_w_in[:, :, f0 + nh:].astype(BF16)
    w_mix_f = jnp.pad(jnp.tile(mix_w_in[:, :, f0:f0 + nh].astype(BF16), (1, 1, DECAY_COPIES)),
                      ((0, 0), (0, 0), (0, f_pad)))
    fbias = jnp.pad(jnp.tile(fgate_b, (1, DECAY_COPIES)), ((0, 0), (0, f_pad)))
    w1_in, w1_out = ffn1_w_in.astype(BF16), ffn1_w_out.astype(BF16)
    w2_in, w2_out = ffn2_w_in.astype(BF16), ffn2_w_out.astype(BF16)
    w_mix_out = mix_w_out.astype(BF16)
    w_a, w_x = lru_w_a.astype(BF16), lru_w_x.astype(BF16)
    q_scale = LOG2E * HEAD_DIM ** -0.5

    h = x.reshape(batch * seq, d)
    for l in range(depth):
        h = _ffn(h, norm_ffn1, w1_in, w1_out, l)
        conv, q, k, v, lru, f = _mix_in(h, norm_mix, w_mix_main, w_mix_lru, w_mix_f, l, d_conv, d_att, q_scale)
        cum = _fcum(f, fbias[l:l + 1], batch, seq)
        ya = _attention(q, k, v, cum, batch, seq)
        yc, yl = _conv_lru(conv, lru, conv_w[l], lru_conv_w[l], lru_conv_b[l], w_a[l], lru_b_a[l],
                           w_x[l], lru_b_x[l], lru_lambda[l], batch, seq)
        h = _mix_out(h, yc, ya, yl, mix_out_norm, w_mix_out, l)
        h = _ffn(h, norm_ffn2, w2_in, w2_out, l, final_gain=final_norm if l == depth - 1 else None)
    return h.reshape(batch, seq, d)
```

```python
import functools

import jax
import jax.numpy as jnp
from jax import lax
from jax.experimental import pallas as pl
from jax.experimental.pallas import tpu as pltpu

F32 = jnp.float32
BF16 = jnp.bfloat16

EPS = 1e-6
LOG2E = 1.4426950408889634
HEAD_DIM = 128
LRU_BLOCKS = 4
LRU_C = 8.0
CONV_WIDTH = 3
LRU_CONV_WIDTH = 4
LANES = 128
SUBLANES = 8
VMEM_LIMIT = 58 * 1024 * 1024


def _params(n_axes, vmem=VMEM_LIMIT):
    return pltpu.CompilerParams(dimension_semantics=("arbitrary",) * n_axes,
                                vmem_limit_bytes=vmem)


def _rms(x, gain):
    ms = jnp.mean(x * x, axis=-1, keepdims=True)
    return (x * lax.rsqrt(ms + EPS)) * gain


def _cast_pieces(src_refs, dst_refs):
    for src, dst in zip(src_refs, dst_refs):
        dst[...] = src[...].astype(dst.dtype)


def _ffn_kernel(x_ref, g_ref, wg_ref, wu_ref, wo_ref, *rest, first_step_chunks, has_final, n_cast):
    rest = list(rest)
    gf_ref = rest.pop(0) if has_final else None
    cast_src, o_ref, cast_dst, xn_ref = rest[:n_cast], rest[n_cast], rest[n_cast + 1:2 * n_cast + 1], rest[-1]
    _cast_pieces(cast_src, cast_dst)
    k = pl.program_id(1)

    def swiglu_half(xn):
        g = jnp.dot(xn, wg_ref[...], preferred_element_type=F32)
        u = jnp.dot(xn, wu_ref[...], preferred_element_type=F32)
        h = (((g * jax.nn.sigmoid(g)) * u) * 0.5).astype(BF16)
        return jnp.dot(h, wo_ref[...], preferred_element_type=F32)

    @pl.when(k == 0)
    def _():
        rc = x_ref.shape[0] // first_step_chunks
        for c in range(first_step_chunks):
            rows = slice(c * rc, (c + 1) * rc)
            x = x_ref[rows, :]
            xn = _rms(x, g_ref[...]).astype(BF16)
            xn_ref[rows, :] = xn
            o_ref[rows, :] = x + swiglu_half(xn)

    @pl.when(k != 0)
    def _():
        o_ref[...] += swiglu_half(xn_ref[...])

    if gf_ref is not None:
        @pl.when(k == pl.num_programs(1) - 1)
        def _():
            o_ref[...] = _rms(o_ref[...], gf_ref[...])


def _ffn(x, gain, w_in, w_out, layer, next_w=None, final_gain=None, tm=1024, tf=512, first_step_chunks=4):
    n, d = x.shape
    f = w_out.shape[0]
    nt, nk = n // tm, f // tf
    in_specs = [
        pl.BlockSpec((tm, d), lambda i, k: (i, 0)),
        pl.BlockSpec((None, 1, d), lambda i, k: (layer, 0, 0)),
        pl.BlockSpec((d, tf), lambda i, k: (0, k)),
        pl.BlockSpec((d, tf), lambda i, k: (0, k + nk)),
        pl.BlockSpec((tf, d), lambda i, k: (k, 0)),
    ]
    args = [x, gain.reshape(gain.shape[0], 1, d), w_in, w_in, w_out]
    out_specs = [pl.BlockSpec((tm, d), lambda i, k: (i, 0))]
    out_shape = [jax.ShapeDtypeStruct((n, d), F32)]
    if final_gain is not None:
        in_specs.append(pl.BlockSpec((1, d), lambda i, k: (0, 0)))
        args.append(final_gain.reshape(1, d))
    if next_w is not None:
        in_tile = (d // nt, 2 * f // nk)
        out_slab = (f // (nt * nk), d)
        in_specs += [pl.BlockSpec((None,) + in_tile, lambda i, k: (layer + 1, i, k)),
                     pl.BlockSpec((None,) + out_slab, lambda i, k: (layer + 1, i * nk + k, 0))]
        args += list(next_w)
        out_specs += [pl.BlockSpec(in_tile, lambda i, k: (i, k)),
                      pl.BlockSpec(out_slab, lambda i, k: (i * nk + k, 0))]
        out_shape += [jax.ShapeDtypeStruct((d, 2 * f), BF16), jax.ShapeDtypeStruct((f, d), BF16)]
    return pl.pallas_call(
        functools.partial(_ffn_kernel, first_step_chunks=first_step_chunks,
                          has_final=final_gain is not None, n_cast=0 if next_w is None else 2),
        grid=(nt, nk),
        in_specs=in_specs,
        out_specs=out_specs,
        out_shape=out_shape,
        scratch_shapes=[pltpu.VMEM((tm, d), BF16)],
        compiler_params=_params(2),
        name="ffn",
    )(*args)


def _mix_in_kernel(x_ref, g_ref, wm_ref, wl_ref, wf_ref, fb_ref, *rest, chunk, q_scale, tiles_per_seq, n_cast):
    cast_src, rest = rest[:n_cast], rest[n_cast:]
    conv_ref, q_ref, k_ref, v_ref, lru_ref, cum_ref = rest[:6]
    cast_dst, carry_ref, scan_ref = rest[6:6 + n_cast], rest[-2], rest[-1]
    _cast_pieces(cast_src, cast_dst)
    xn = _rms(x_ref[...], g_ref[...]).astype(BF16)

    def project(w_ref, col, out):
        for c in range(0, out.shape[1], chunk):
            w = min(chunk, out.shape[1] - c)
            y = jnp.dot(xn, w_ref[:, col + c:col + c + w], preferred_element_type=F32)
            if out is q_ref:
                y = y * q_scale
            out[:, c:c + w] = y.astype(out.dtype)
        return col + out.shape[1]

    col = 0
    for out in (conv_ref, q_ref, k_ref, v_ref):
        col = project(wm_ref, col, out)
    project(wl_ref, 0, lru_ref)

    @pl.when(pl.program_id(0) % tiles_per_seq == 0)
    def _():
        carry_ref[...] = jnp.zeros_like(carry_ref)

    z = jnp.dot(xn, wf_ref[...], preferred_element_type=F32) + fb_ref[...]
    t = z.shape[0]
    pad = scan_ref.shape[0] - t
    rows = slice(pad, pad + t)
    scan_ref[0:pad, :] = jnp.zeros((pad, z.shape[1]), F32)
    scan_ref[rows, :] = -(jnp.maximum(-z, 0.0) + jnp.log1p(jnp.exp(-jnp.abs(z))))
    shift = 1
    while shift < t:
        scan_ref[rows, :] = scan_ref[rows, :] + scan_ref[pad - shift:pad - shift + t, :]
        shift *= 2
    cum = scan_ref[rows, :] + carry_ref[0:1, :]
    cum_ref[...] = cum
    carry_ref[...] = jnp.broadcast_to(cum[t - 1:t, :], carry_ref.shape)


def _row_slab_cast(w_stack, layer, steps):
    slab = (w_stack.shape[1] // steps, w_stack.shape[2])
    assert slab[0] * steps == w_stack.shape[1] and slab[0] % (2 * SUBLANES) == 0
    return (pl.BlockSpec((None,) + slab, lambda i: (layer, i, 0)), pl.BlockSpec(slab, lambda i: (i, 0)),
            jax.ShapeDtypeStruct(w_stack.shape[1:], BF16))


def _mix_in(x, gain, w_all, w_lru, w_f, f_bias, layer, d_conv, d_att, q_scale, seq, next_w=None,
            tm=256, chunk=512):
    n, d = x.shape
    main = 3 * d_conv + 3 * d_att
    widths = (3 * d_conv, d_att, d_att, d_att, w_lru.shape[1], w_f.shape[1])
    dtypes = (F32, BF16, BF16, BF16, F32, F32)
    resident = lambda shape: pl.BlockSpec(shape, lambda i: (0, 0), pipeline_mode=pl.Buffered(1))
    in_specs = [
        pl.BlockSpec((tm, d), lambda i: (i, 0)),
        pl.BlockSpec((None, 1, d), lambda i: (layer, 0, 0)),
        resident((d, main)), resident(w_lru.shape), resident(w_f.shape),
        pl.BlockSpec((None, 1, w_f.shape[1]), lambda i: (layer, 0, 0)),
    ]
    args = [x, gain.reshape(gain.shape[0], 1, d), w_all, w_lru, w_f, f_bias.reshape(f_bias.shape[0], 1, -1)]
    out_specs = [pl.BlockSpec((tm, wd), lambda i: (i, 0)) for wd in widths]
    out_shape = [jax.ShapeDtypeStruct((n, wd), dt) for wd, dt in zip(widths, dtypes)]
    if next_w is not None:
        src_spec, dst_spec, dst_shape = _row_slab_cast(next_w, layer + 1, n // tm)
        in_specs.append(src_spec)
        args.append(next_w)
        out_specs.append(dst_spec)
        out_shape.append(dst_shape)
    return pl.pallas_call(
        functools.partial(_mix_in_kernel, chunk=chunk, q_scale=q_scale, tiles_per_seq=seq // tm,
                          n_cast=0 if next_w is None else 1),
        grid=(n // tm,),
        in_specs=in_specs,
        out_specs=out_specs,
        out_shape=out_shape,
        scratch_shapes=[pltpu.VMEM((SUBLANES, w_f.shape[1]), F32),
                        pltpu.VMEM((tm // 2 + tm, w_f.shape[1]), F32)],
        compiler_params=_params(1),
        name="mix_in",
    )(*args)


DECAY_COPIES = 6


def _decay_split(c_block):
    c = c_block * LOG2E
    hi = c.astype(BF16).astype(F32)
    rest = c - hi
    mid = rest.astype(BF16).astype(F32)
    return hi, mid, rest - mid


def _decay_columns(pieces, head, nh, on_query_side):
    hi, mid, lo = pieces
    lane = lax.broadcasted_iota(jnp.int32, hi.shape, 1)
    onehot = jnp.where(lane % nh == head, 1.0, 0.0)
    first = 0 if on_query_side else 3 * nh
    split = jnp.where(lane < first + nh, hi, jnp.where(lane < first + 2 * nh, mid, lo))
    if on_query_side:
        out = jnp.where(lane < 3 * nh, split, jnp.where(lane < 6 * nh, onehot, 0.0))
    else:
        out = jnp.where(lane < 3 * nh, onehot, jnp.where(lane < 6 * nh, -split, 0.0))
    return out.astype(BF16)


def _attn_kernel(q_ref, k_ref, v_ref, cq_ref, ck_ref, o_ref, kx_ref, *, tq, nh, build_rows):
    i = pl.program_id(2)
    dh = HEAD_DIM
    hp = q_ref.shape[1] // dh
    heads = [(pl.program_id(1) * hp + g, slice(g * dh, (g + 1) * dh)) for g in range(hp)]

    @pl.when(i == 0)
    def _():
        def build(c, carry):
            rows = pl.ds(pl.multiple_of(c * build_rows, build_rows), build_rows)
            pieces = _decay_split(ck_ref[rows, :])
            for g, (head, cols) in enumerate(heads):
                kx_ref[g, rows, 0:dh] = k_ref[rows, cols]
                kx_ref[g, rows, dh:2 * dh] = _decay_columns(pieces, head, nh, on_query_side=False)
            return carry
        lax.fori_loop(0, k_ref.shape[0] // build_rows, build, 0)

    pieces = _decay_split(cq_ref[...])
    qx_t = []
    for head, cols in heads:
        qx = jnp.concatenate([q_ref[:, cols].astype(F32),
                              _decay_columns(pieces, head, nh, on_query_side=True).astype(F32)], axis=1)
        qx_t.append(qx.T.astype(BF16))

    def update(j, carries, masked):
        rows = pl.ds(pl.multiple_of(j * tq, tq), tq)
        scores = [jnp.dot(kx_ref[g, rows, :], qx_t[g], preferred_element_type=F32)
                  for g in range(hp)]
        probs, new = [], []
        for g, s in enumerate(scores):
            m, l, acc = carries[g]
            if masked:
                r = lax.broadcasted_iota(jnp.int32, s.shape, 0)
                c = lax.broadcasted_iota(jnp.int32, s.shape, 1)
                s = jnp.where(r <= c, s, -jnp.inf)
            m_new = jnp.maximum(m, jnp.max(s, axis=0, keepdims=True))
            alpha = jnp.exp2(m - m_new)
            p = jnp.exp2(s - m_new)
            probs.append(p.astype(BF16))
            new.append((m_new, alpha * l + jnp.sum(p, axis=0, keepdims=True), alpha * acc))
        out = []
        for g, p in enumerate(probs):
            pv = lax.dot_general(v_ref[rows, heads[g][1]], p, (((0,), (0,)), ((), ())),
                                 preferred_element_type=F32)
            out.append((new[g][0], new[g][1], new[g][2] + pv))
        return tuple(out)

    init = (jnp.full((1, tq), -jnp.inf, F32), jnp.zeros((1, tq), F32), jnp.zeros((dh, tq), F32))
    carries = lax.fori_loop(0, i, functools.partial(update, masked=False), (init,) * hp)
    carries = update(i, carries, masked=True)
    for g, (_, cols) in enumerate(heads):
        m, l, acc = carries[g]
        o_ref[:, cols] = (acc * (1.0 / l)).T


def _attention(q, k, v, cum, batch, seq, tq=512, hp=4, build_rows=512):
    n, d_att = q.shape
    nh = d_att // HEAD_DIM
    nq = seq // tq
    width = hp * HEAD_DIM
    return pl.pallas_call(
        functools.partial(_attn_kernel, tq=tq, nh=nh, build_rows=build_rows),
        grid=(batch, nh // hp, nq),
        in_specs=[
            pl.BlockSpec((tq, width), lambda b, h, i: (b * nq + i, h)),
            pl.BlockSpec((seq, width), lambda b, h, i: (b, h)),
            pl.BlockSpec((seq, width), lambda b, h, i: (b, h)),
            pl.BlockSpec((tq, LANES), lambda b, h, i: (b * nq + i, 0)),
            pl.BlockSpec((seq, LANES), lambda b, h, i: (b, 0)),
        ],
        out_specs=pl.BlockSpec((tq, width), lambda b, h, i: (b * nq + i, h)),
        out_shape=jax.ShapeDtypeStruct((n, d_att), F32),
        scratch_shapes=[pltpu.VMEM((hp, seq, 2 * HEAD_DIM), BF16)],
        compiler_params=_params(3),
        name="fox_attention",
    )(q, k, v, cum, cum)


def _gelu_tanh(x):
    return 0.5 * x * (1.0 + jnp.tanh(0.7978845608028654 * (x + 0.044715 * (x * x * x))))


def _conv_lru_kernel(conv_ref, lru_ref, cw_ref, lw_ref, lb_ref, wa_ref, ba_ref, wx_ref, bx_ref,
                     lam_ref, yc_ref, yl_ref, zbuf, xbuf, abuf, ubuf, h_ref, *, tt, dc, dl, pad):
    @pl.when(pl.program_id(1) == 0)
    def _():
        zbuf[0:SUBLANES, :] = jnp.zeros((SUBLANES, dc), F32)
        xbuf[0:SUBLANES, :] = jnp.zeros((SUBLANES, dl), F32)
        h_ref[...] = jnp.zeros_like(h_ref)
        abuf[0:pad, :] = jnp.ones((pad, dl), F32)
        ubuf[0:pad, :] = jnp.zeros((pad, dl), F32)

    o = SUBLANES
    zbuf[o:o + tt, :] = conv_ref[:, dc:2 * dc] * conv_ref[:, 2 * dc:3 * dc]
    yc = cw_ref[CONV_WIDTH - 1:CONV_WIDTH, :] * zbuf[o:o + tt, :]
    for s in range(1, CONV_WIDTH):
        yc = yc + cw_ref[CONV_WIDTH - 1 - s:CONV_WIDTH - s, :] * zbuf[o - s:o - s + tt, :]
    yc_ref[...] = conv_ref[:, 0:dc] * yc
    zbuf[0:o, :] = zbuf[tt:tt + o, :]

    xbuf[o:o + tt, :] = lru_ref[:, dl:2 * dl]
    xr = lw_ref[LRU_CONV_WIDTH - 1:LRU_CONV_WIDTH, :] * xbuf[o:o + tt, :]
    for s in range(1, LRU_CONV_WIDTH):
        xr = xr + lw_ref[LRU_CONV_WIDTH - 1 - s:LRU_CONV_WIDTH - s, :] * xbuf[o - s:o - s + tt, :]
    xr = xr + lb_ref[...]
    xbuf[0:o, :] = xbuf[tt:tt + o, :]

    xb = xr.astype(BF16)
    blk = dl // LRU_BLOCKS
    lam = lam_ref[...]
    softplus_neg_lam = jnp.maximum(-lam, 0.0) + jnp.log1p(jnp.exp(-jnp.abs(lam)))
    for g in range(LRU_BLOCKS):
        sl = slice(g * blk, (g + 1) * blk)
        r = jax.nn.sigmoid(jnp.dot(xb[:, sl], wa_ref[g], preferred_element_type=F32) + ba_ref[:, sl])
        ig = jax.nn.sigmoid(jnp.dot(xb[:, sl], wx_ref[g], preferred_element_type=F32) + bx_ref[:, sl])
        log_a = (-LRU_C * r) * softplus_neg_lam[:, sl]
        a = jnp.exp(log_a)
        th = jnp.tanh(log_a)
        mult = jnp.sqrt(-2.0 * th / (1.0 - th))
        abuf[pad:pad + tt, sl] = a
        ubuf[pad:pad + tt, sl] = mult * (ig * xr[:, sl])

    s = 1
    while s < tt:
        a = abuf[pad:pad + tt, :]
        u = ubuf[pad:pad + tt, :]
        a_sh = abuf[pad - s:pad - s + tt, :]
        u_sh = ubuf[pad - s:pad - s + tt, :]
        ubuf[pad:pad + tt, :] = a * u_sh + u
        abuf[pad:pad + tt, :] = a * a_sh
        s *= 2
    hs = abuf[pad:pad + tt, :] * h_ref[0:1, :] + ubuf[pad:pad + tt, :]
    h_ref[...] = jnp.broadcast_to(hs[tt - 1:tt, :], h_ref.shape)
    yl_ref[...] = _gelu_tanh(lru_ref[:, 0:dl]) * hs


def _conv_lru(conv, lru, conv_w, lru_conv_w, lru_conv_b, w_a, b_a, w_x, b_x, lam, batch, seq, tt=256):
    n = conv.shape[0]
    dc = conv.shape[1] // 3
    dl = lru.shape[1] // 2
    nt = seq // tt
    pad = tt // 2
    row = lambda b, t: (b * nt + t, 0)
    fixed2 = lambda b, t: (0, 0)
    fixed3 = lambda b, t: (0, 0, 0)
    return pl.pallas_call(
        functools.partial(_conv_lru_kernel, tt=tt, dc=dc, dl=dl, pad=pad),
        grid=(batch, nt),
        in_specs=[
            pl.BlockSpec((tt, 3 * dc), row),
            pl.BlockSpec((tt, 2 * dl), row),
            pl.BlockSpec(conv_w.shape, fixed2),
            pl.BlockSpec(lru_conv_w.shape, fixed2),
            pl.BlockSpec((1, dl), fixed2),
            pl.BlockSpec(w_a.shape, fixed3),
            pl.BlockSpec((1, dl), fixed2),
            pl.BlockSpec(w_x.shape, fixed3),
            pl.BlockSpec((1, dl), fixed2),
            pl.BlockSpec((1, dl), fixed2),
        ],
        out_specs=[pl.BlockSpec((tt, dc), row), pl.BlockSpec((tt, dl), row)],
        out_shape=[jax.ShapeDtypeStruct((n, dc), F32), jax.ShapeDtypeStruct((n, dl), F32)],
        scratch_shapes=[
            pltpu.VMEM((tt + SUBLANES, dc), F32),
            pltpu.VMEM((tt + SUBLANES, dl), F32),
            pltpu.VMEM((tt + pad, dl), F32),
            pltpu.VMEM((tt + pad, dl), F32),
            pltpu.VMEM((SUBLANES, dl), F32),
        ],
        compiler_params=_params(2),
        name="conv_lru",
    )(conv, lru, conv_w, lru_conv_w, lru_conv_b.reshape(1, dl), w_a, b_a.reshape(1, dl),
      w_x, b_x.reshape(1, dl), lam.reshape(1, dl))


def _mix_out_kernel(x_ref, yc_ref, ya_ref, yl_ref, g_ref, w_ref, *rest, n_cast):
    cast_src, o_ref, cast_dst, y_ref = rest[:n_cast], rest[n_cast], rest[n_cast + 1:2 * n_cast + 1], rest[-1]
    _cast_pieces(cast_src, cast_dst)
    col = 0
    for src in (yc_ref, ya_ref, yl_ref):
        width = src.shape[1]
        y_ref[:, col:col + width] = _rms(src[...], g_ref[:, col:col + width]).astype(BF16)
        col += width
    o_ref[...] = x_ref[...] + jnp.dot(y_ref[...], w_ref[...], preferred_element_type=F32)


def _mix_out(x, yc, ya, yl, gain, w, layer, next_w=None, tm=512):
    n, d = x.shape
    dm = w.shape[0]
    row = lambda i: (i, 0)
    in_specs = [
        pl.BlockSpec((tm, d), row),
        pl.BlockSpec((tm, yc.shape[1]), row),
        pl.BlockSpec((tm, ya.shape[1]), row),
        pl.BlockSpec((tm, yl.shape[1]), row),
        pl.BlockSpec((None, 1, dm), lambda i: (layer, 0, 0)),
        pl.BlockSpec(w.shape, lambda i: (0, 0), pipeline_mode=pl.Buffered(1)),
    ]
    args = [x, yc, ya, yl, gain.reshape(gain.shape[0], 1, dm), w]
    out_specs = [pl.BlockSpec((tm, d), row)]
    out_shape = [jax.ShapeDtypeStruct((n, d), F32)]
    if next_w is not None:
        src_spec, dst_spec, dst_shape = _row_slab_cast(next_w, layer + 1, n // tm)
        in_specs.append(src_spec)
        args.append(next_w)
        out_specs.append(dst_spec)
        out_shape.append(dst_shape)
    return pl.pallas_call(
        functools.partial(_mix_out_kernel, n_cast=0 if next_w is None else 1),
        grid=(n // tm,),
        in_specs=in_specs,
        out_specs=out_specs,
        out_shape=out_shape,
        scratch_shapes=[pltpu.VMEM((tm, dm), BF16)],
        compiler_params=_params(1),
        name="mix_out",
    )(*args)


def kernel(x, norm_ffn1, ffn1_w_in, ffn1_w_out, norm_mix, mix_w_in, conv_w, fgate_b, lru_conv_w,
           lru_conv_b, lru_w_a, lru_b_a, lru_w_x, lru_b_x, lru_lambda, mix_out_norm, mix_w_out,
           norm_ffn2, ffn2_w_in, ffn2_w_out, final_norm):
    batch, seq, d = x.shape
    depth = norm_ffn1.shape[0]
    d_conv = conv_w.shape[2]
    d_lru = lru_lambda.shape[1]
    d_att = mix_w_out.shape[1] - d_conv - d_lru
    nh = d_att // HEAD_DIM
    assert nh == fgate_b.shape[1] and DECAY_COPIES * nh <= LANES

    f0 = 3 * d_conv + 3 * d_att
    f_pad = LANES - DECAY_COPIES * nh
    fbias = jnp.pad(jnp.tile(fgate_b, (1, DECAY_COPIES)), ((0, 0), (0, f_pad)))
    w_a, w_x = lru_w_a.astype(BF16), lru_w_x.astype(BF16)
    q_scale = LOG2E * HEAD_DIM ** -0.5

    w1 = [ffn1_w_in[0].astype(BF16), ffn1_w_out[0].astype(BF16)]
    w2 = [ffn2_w_in[0].astype(BF16), ffn2_w_out[0].astype(BF16)]
    w_mix = mix_w_in[0].astype(BF16)
    w_mix_out = mix_w_out[0].astype(BF16)

    h = x.reshape(batch * seq, d)
    for l in range(depth):
        more = l + 1 < depth
        h, *w1_next = _ffn(h, norm_ffn1, w1[0], w1[1], l, next_w=(ffn1_w_in, ffn1_w_out) if more else None)
        w_mix_lru = w_mix[:, f0 + nh:]
        w_mix_f = jnp.pad(jnp.tile(w_mix[:, f0:f0 + nh], (1, DECAY_COPIES)), ((0, 0), (0, f_pad)))
        conv, q, k, v, lru, cum, *w_mix_next = _mix_in(
            h, norm_mix, w_mix, w_mix_lru, w_mix_f, fbias, l, d_conv, d_att, q_scale, seq,
            next_w=mix_w_in if more else None)
        ya = _attention(q, k, v, cum, batch, seq)
        yc, yl = _conv_lru(conv, lru, conv_w[l], lru_conv_w[l], lru_conv_b[l], w_a[l], lru_b_a[l],
                           w_x[l], lru_b_x[l], lru_lambda[l], batch, seq)
        h, *w_mix_out_next = _mix_out(h, yc, ya, yl, mix_out_norm, w_mix_out, l,
                                      next_w=mix_w_out if more else None)
        h, *w2_next = _ffn(h, norm_ffn2, w2[0], w2[1], l, next_w=(ffn2_w_in, ffn2_w_out) if more else None,
                           final_gain=None if more else final_norm)
        if more:
            w1, w2, w_mix, w_mix_out = w1_next, w2_next, w_mix_next[0], w_mix_out_next[0]
    return h.reshape(batch, seq, d)
```

```python
import functools

import jax
import jax.numpy as jnp
from jax import lax
from jax.experimental import pallas as pl
from jax.experimental.pallas import tpu as pltpu

F32 = jnp.float32
BF16 = jnp.bfloat16

EPS = 1e-6
LOG2E = 1.4426950408889634
HEAD_DIM = 128
LRU_BLOCKS = 4
LRU_C = 8.0
CONV_WIDTH = 3
LRU_CONV_WIDTH = 4
LANES = 128
SUBLANES = 8
VMEM_LIMIT = 58 * 1024 * 1024


def _params(n_axes, vmem=VMEM_LIMIT, **extra):
    return pltpu.CompilerParams(dimension_semantics=("arbitrary",) * n_axes,
                                vmem_limit_bytes=vmem, **extra)


def _rms(x, gain):
    ms = jnp.mean(x * x, axis=-1, keepdims=True)
    return (x * lax.rsqrt(ms + EPS)) * gain


def _cast_pieces(src_refs, dst_refs):
    for src, dst in zip(src_refs, dst_refs):
        dst[...] = src[...].astype(dst.dtype)


def _ffn_kernel(x_ref, g_ref, wg_ref, wu_ref, wo_ref, *rest, first_step_chunks, has_final, n_cast):
    rest = list(rest)
    gf_ref = rest.pop(0) if has_final else None
    cast_src, o_ref, cast_dst, xn_ref = rest[:n_cast], rest[n_cast], rest[n_cast + 1:2 * n_cast + 1], rest[-1]
    _cast_pieces(cast_src, cast_dst)
    k = pl.program_id(1)

    def swiglu_half(xn):
        g = jnp.dot(xn, wg_ref[...], preferred_element_type=F32)
        u = jnp.dot(xn, wu_ref[...], preferred_element_type=F32)
        h = (((g * jax.nn.sigmoid(g)) * u) * 0.5).astype(BF16)
        return jnp.dot(h, wo_ref[...], preferred_element_type=F32)

    @pl.when(k == 0)
    def _():
        rc = x_ref.shape[0] // first_step_chunks
        for c in range(first_step_chunks):
            rows = slice(c * rc, (c + 1) * rc)
            x = x_ref[rows, :]
            xn = _rms(x, g_ref[...]).astype(BF16)
            xn_ref[rows, :] = xn
            o_ref[rows, :] = x + swiglu_half(xn)

    @pl.when(k != 0)
    def _():
        o_ref[...] += swiglu_half(xn_ref[...])

    if gf_ref is not None:
        @pl.when(k == pl.num_programs(1) - 1)
        def _():
            o_ref[...] = _rms(o_ref[...], gf_ref[...])


def _ffn(x, gain, w_in, w_out, layer, next_w=None, final_gain=None, tm=1024, tf=512, first_step_chunks=4):
    n, d = x.shape
    f = w_out.shape[0]
    nt, nk = n // tm, f // tf
    in_specs = [
        pl.BlockSpec((tm, d), lambda i, k: (i, 0)),
        pl.BlockSpec((None, 1, d), lambda i, k: (layer, 0, 0)),
        pl.BlockSpec((d, tf), lambda i, k: (0, k)),
        pl.BlockSpec((d, tf), lambda i, k: (0, k + nk)),
        pl.BlockSpec((tf, d), lambda i, k: (k, 0)),
    ]
    args = [x, gain.reshape(gain.shape[0], 1, d), w_in, w_in, w_out]
    out_specs = [pl.BlockSpec((tm, d), lambda i, k: (i, 0))]
    out_shape = [jax.ShapeDtypeStruct((n, d), F32)]
    if final_gain is not None:
        in_specs.append(pl.BlockSpec((1, d), lambda i, k: (0, 0)))
        args.append(final_gain.reshape(1, d))
    if next_w is not None:
        in_tile = (d // nt, 2 * f // nk)
        out_slab = (f // (nt * nk), d)
        in_specs += [pl.BlockSpec((None,) + in_tile, lambda i, k: (layer + 1, i, k)),
                     pl.BlockSpec((None,) + out_slab, lambda i, k: (layer + 1, i * nk + k, 0))]
        args += list(next_w)
        out_specs += [pl.BlockSpec(in_tile, lambda i, k: (i, k)),
                      pl.BlockSpec(out_slab, lambda i, k: (i * nk + k, 0))]
        out_shape += [jax.ShapeDtypeStruct((d, 2 * f), BF16), jax.ShapeDtypeStruct((f, d), BF16)]
    return pl.pallas_call(
        functools.partial(_ffn_kernel, first_step_chunks=first_step_chunks,
                          has_final=final_gain is not None, n_cast=0 if next_w is None else 2),
        grid=(nt, nk),
        in_specs=in_specs,
        out_specs=out_specs,
        out_shape=out_shape,
        scratch_shapes=[pltpu.VMEM((tm, d), BF16)],
        compiler_params=_params(2),
        name="ffn",
    )(*args)


def _mix_in_kernel(x_ref, g_ref, wm_ref, wl_ref, wf_ref, fb_ref, conv_ref, q_ref, k_ref, v_ref, lru_ref,
                   cum_ref, carry_ref, scan_ref, *, chunk, q_scale, tiles_per_seq):
    xn = _rms(x_ref[...], g_ref[...]).astype(BF16)

    def project(w_ref, col, out):
        for c in range(0, out.shape[1], chunk):
            w = min(chunk, out.shape[1] - c)
            y = jnp.dot(xn, w_ref[:, col + c:col + c + w], preferred_element_type=F32)
            if out is q_ref:
                y = y * q_scale
            out[:, c:c + w] = y.astype(out.dtype)
        return col + out.shape[1]

    col = 0
    for out in (conv_ref, q_ref, k_ref, v_ref):
        col = project(wm_ref, col, out)
    project(wl_ref, 0, lru_ref)

    @pl.when(pl.program_id(0) % tiles_per_seq == 0)
    def _():
        carry_ref[...] = jnp.zeros_like(carry_ref)

    z = jnp.dot(xn, wf_ref[...], preferred_element_type=F32) + fb_ref[...]
    t = z.shape[0]
    pad = scan_ref.shape[0] - t
    rows = slice(pad, pad + t)
    scan_ref[0:pad, :] = jnp.zeros((pad, z.shape[1]), F32)
    scan_ref[rows, :] = -(jnp.maximum(-z, 0.0) + jnp.log1p(jnp.exp(-jnp.abs(z))))
    shift = 1
    while shift < t:
        scan_ref[rows, :] = scan_ref[rows, :] + scan_ref[pad - shift:pad - shift + t, :]
        shift *= 2
    cum = scan_ref[rows, :] + carry_ref[0:1, :]
    cum_ref[...] = cum
    carry_ref[...] = jnp.broadcast_to(cum[t - 1:t, :], carry_ref.shape)


def _row_slab_cast(w_stack, layer, steps):
    slab = (w_stack.shape[1] // steps, w_stack.shape[2])
    assert slab[0] * steps == w_stack.shape[1] and slab[0] % (2 * SUBLANES) == 0
    return (pl.BlockSpec((None,) + slab, lambda i: (layer, i, 0)), pl.BlockSpec(slab, lambda i: (i, 0)),
            jax.ShapeDtypeStruct(w_stack.shape[1:], BF16))


def _mix_in(x, gain, w_all, w_lru, w_f, f_bias, layer, d_conv, d_att, q_scale, seq, tm=256, chunk=512):
    n, d = x.shape
    main = 3 * d_conv + 3 * d_att
    widths = (3 * d_conv, d_att, d_att, d_att, w_lru.shape[2], w_f.shape[2])
    dtypes = (F32, BF16, BF16, BF16, F32, F32)
    resident = lambda shape: pl.BlockSpec((None,) + shape, lambda i: (layer, 0, 0),
                                          pipeline_mode=pl.Buffered(1))
    return pl.pallas_call(
        functools.partial(_mix_in_kernel, chunk=chunk, q_scale=q_scale, tiles_per_seq=seq // tm),
        grid=(n // tm,),
        in_specs=[
            pl.BlockSpec((tm, d), lambda i: (i, 0)),
            pl.BlockSpec((None, 1, d), lambda i: (layer, 0, 0)),
            resident((d, main)), resident(w_lru.shape[1:]), resident(w_f.shape[1:]),
            pl.BlockSpec((None, 1, w_f.shape[2]), lambda i: (layer, 0, 0)),
        ],
        out_specs=[pl.BlockSpec((tm, wd), lambda i: (i, 0)) for wd in widths],
        out_shape=[jax.ShapeDtypeStruct((n, wd), dt) for wd, dt in zip(widths, dtypes)],
        scratch_shapes=[pltpu.VMEM((SUBLANES, w_f.shape[2]), F32),
                        pltpu.VMEM((tm // 2 + tm, w_f.shape[2]), F32)],
        compiler_params=_params(1),
        name="mix_in",
    )(x, gain.reshape(gain.shape[0], 1, d), w_all, w_lru, w_f, f_bias.reshape(f_bias.shape[0], 1, -1))


DECAY_COPIES = 6


def _decay_split(c_block):
    c = c_block * LOG2E
    hi = c.astype(BF16).astype(F32)
    rest = c - hi
    mid = rest.astype(BF16).astype(F32)
    return hi, mid, rest - mid


def _decay_columns(pieces, head, nh, on_query_side):
    hi, mid, lo = pieces
    lane = lax.broadcasted_iota(jnp.int32, hi.shape, 1)
    onehot = jnp.where(lane % nh == head, 1.0, 0.0)
    first = 0 if on_query_side else 3 * nh
    split = jnp.where(lane < first + nh, hi, jnp.where(lane < first + 2 * nh, mid, lo))
    if on_query_side:
        out = jnp.where(lane < 3 * nh, split, jnp.where(lane < 6 * nh, onehot, 0.0))
    else:
        out = jnp.where(lane < 3 * nh, onehot, jnp.where(lane < 6 * nh, -split, 0.0))
    return out.astype(BF16)


def _attn_kernel(q_ref, k_ref, v_ref, cq_ref, ck_ref, o_ref, kx_ref, *, tq, nh, build_rows):
    i = pl.program_id(2)
    dh = HEAD_DIM
    hp = q_ref.shape[1] // dh
    heads = [(pl.program_id(1) * hp + g, slice(g * dh, (g + 1) * dh)) for g in range(hp)]

    @pl.when(i == 0)
    def _():
        def build(c, carry):
            rows = pl.ds(pl.multiple_of(c * build_rows, build_rows), build_rows)
            pieces = _decay_split(ck_ref[rows, :])
            for g, (head, cols) in enumerate(heads):
                kx_ref[g, rows, 0:dh] = k_ref[rows, cols]
                kx_ref[g, rows, dh:2 * dh] = _decay_columns(pieces, head, nh, on_query_side=False)
            return carry
        lax.fori_loop(0, k_ref.shape[0] // build_rows, build, 0)

    pieces = _decay_split(cq_ref[...])
    qx_t = []
    for head, cols in heads:
        qx = jnp.concatenate([q_ref[:, cols].astype(F32),
                              _decay_columns(pieces, head, nh, on_query_side=True).astype(F32)], axis=1)
        qx_t.append(qx.T.astype(BF16))

    def update(j, carries, masked):
        rows = pl.ds(pl.multiple_of(j * tq, tq), tq)
        scores = [jnp.dot(kx_ref[g, rows, :], qx_t[g], preferred_element_type=F32)
                  for g in range(hp)]
        probs, new = [], []
        for g, s in enumerate(scores):
            m, l, acc = carries[g]
            if masked:
                r = lax.broadcasted_iota(jnp.int32, s.shape, 0)
                c = lax.broadcasted_iota(jnp.int32, s.shape, 1)
                s = jnp.where(r <= c, s, -jnp.inf)
            m_new = jnp.maximum(m, jnp.max(s, axis=0, keepdims=True))
            alpha = jnp.exp2(m - m_new)
            p = jnp.exp2(s - m_new)
            probs.append(p.astype(BF16))
            new.append((m_new, alpha * l + jnp.sum(p, axis=0, keepdims=True), alpha * acc))
        out = []
        for g, p in enumerate(probs):
            pv = lax.dot_general(v_ref[rows, heads[g][1]], p, (((0,), (0,)), ((), ())),
                                 preferred_element_type=F32)
            out.append((new[g][0], new[g][1], new[g][2] + pv))
        return tuple(out)

    init = (jnp.full((1, tq), -jnp.inf, F32), jnp.zeros((1, tq), F32), jnp.zeros((dh, tq), F32))
    carries = lax.fori_loop(0, i, functools.partial(update, masked=False), (init,) * hp)
    carries = update(i, carries, masked=True)
    for g, (_, cols) in enumerate(heads):
        m, l, acc = carries[g]
        o_ref[:, cols] = (acc * (1.0 / l)).T


def _attention(q, k, v, cum, batch, seq, tq=512, hp=4, build_rows=512):
    n, d_att = q.shape
    nh = d_att // HEAD_DIM
    nq = seq // tq
    width = hp * HEAD_DIM
    return pl.pallas_call(
        functools.partial(_attn_kernel, tq=tq, nh=nh, build_rows=build_rows),
        grid=(batch, nh // hp, nq),
        in_specs=[
            pl.BlockSpec((tq, width), lambda b, h, i: (b * nq + i, h)),
            pl.BlockSpec((seq, width), lambda b, h, i: (b, h)),
            pl.BlockSpec((seq, width), lambda b, h, i: (b, h)),
            pl.BlockSpec((tq, LANES), lambda b, h, i: (b * nq + i, 0)),
            pl.BlockSpec((seq, LANES), lambda b, h, i: (b, 0)),
        ],
        out_specs=pl.BlockSpec((tq, width), lambda b, h, i: (b * nq + i, h)),
        out_shape=jax.ShapeDtypeStruct((n, d_att), F32),
        scratch_shapes=[pltpu.VMEM((hp, seq, 2 * HEAD_DIM), BF16)],
        compiler_params=_params(3),
        name="fox_attention",
    )(q, k, v, cum, cum)


def _gelu_tanh(x):
    return 0.5 * x * (1.0 + jnp.tanh(0.7978845608028654 * (x + 0.044715 * (x * x * x))))


def _conv_lru_kernel(conv_ref, lru_ref, cw_ref, lw_ref, lb_ref, wa_ref, ba_ref, wx_ref, bx_ref,
                     lam_ref, yc_ref, yl_ref, zbuf, xbuf, abuf, ubuf, h_ref, *, tt, dc, dl, pad):
    @pl.when(pl.program_id(1) == 0)
    def _():
        zbuf[0:SUBLANES, :] = jnp.zeros((SUBLANES, dc), F32)
        xbuf[0:SUBLANES, :] = jnp.zeros((SUBLANES, dl), F32)
        h_ref[...] = jnp.zeros_like(h_ref)
        abuf[0:pad, :] = jnp.ones((pad, dl), F32)
        ubuf[0:pad, :] = jnp.zeros((pad, dl), F32)

    o = SUBLANES
    zbuf[o:o + tt, :] = conv_ref[:, dc:2 * dc] * conv_ref[:, 2 * dc:3 * dc]
    yc = cw_ref[CONV_WIDTH - 1:CONV_WIDTH, :] * zbuf[o:o + tt, :]
    for s in range(1, CONV_WIDTH):
        yc = yc + cw_ref[CONV_WIDTH - 1 - s:CONV_WIDTH - s, :] * zbuf[o - s:o - s + tt, :]
    yc_ref[...] = conv_ref[:, 0:dc] * yc
    zbuf[0:o, :] = zbuf[tt:tt + o, :]

    xbuf[o:o + tt, :] = lru_ref[:, dl:2 * dl]
    xr = lw_ref[LRU_CONV_WIDTH - 1:LRU_CONV_WIDTH, :] * xbuf[o:o + tt, :]
    for s in range(1, LRU_CONV_WIDTH):
        xr = xr + lw_ref[LRU_CONV_WIDTH - 1 - s:LRU_CONV_WIDTH - s, :] * xbuf[o - s:o - s + tt, :]
    xr = xr + lb_ref[...]
    xbuf[0:o, :] = xbuf[tt:tt + o, :]

    xb = xr.astype(BF16)
    blk = dl // LRU_BLOCKS
    lam = lam_ref[...]
    softplus_neg_lam = jnp.maximum(-lam, 0.0) + jnp.log1p(jnp.exp(-jnp.abs(lam)))
    for g in range(LRU_BLOCKS):
        sl = slice(g * blk, (g + 1) * blk)
        r = jax.nn.sigmoid(jnp.dot(xb[:, sl], wa_ref[g], preferred_element_type=F32) + ba_ref[:, sl])
        ig = jax.nn.sigmoid(jnp.dot(xb[:, sl], wx_ref[g], preferred_element_type=F32) + bx_ref[:, sl])
        log_a = (-LRU_C * r) * softplus_neg_lam[:, sl]
        a = jnp.exp(log_a)
        th = jnp.tanh(log_a)
        mult = jnp.sqrt(-2.0 * th / (1.0 - th))
        abuf[pad:pad + tt, sl] = a
        ubuf[pad:pad + tt, sl] = mult * (ig * xr[:, sl])

    s = 1
    while s < tt:
        a = abuf[pad:pad + tt, :]
        u = ubuf[pad:pad + tt, :]
        a_sh = abuf[pad - s:pad - s + tt, :]
        u_sh = ubuf[pad - s:pad - s + tt, :]
        ubuf[pad:pad + tt, :] = a * u_sh + u
        abuf[pad:pad + tt, :] = a * a_sh
        s *= 2
    hs = abuf[pad:pad + tt, :] * h_ref[0:1, :] + ubuf[pad:pad + tt, :]
    h_ref[...] = jnp.broadcast_to(hs[tt - 1:tt, :], h_ref.shape)
    yl_ref[...] = _gelu_tanh(lru_ref[:, 0:dl]) * hs


def _conv_lru(conv, lru, conv_w, lru_conv_w, lru_conv_b, w_a, b_a, w_x, b_x, lam, batch, seq, tt=256):
    n = conv.shape[0]
    dc = conv.shape[1] // 3
    dl = lru.shape[1] // 2
    nt = seq // tt
    pad = tt // 2
    row = lambda b, t: (b * nt + t, 0)
    fixed2 = lambda b, t: (0, 0)
    fixed3 = lambda b, t: (0, 0, 0)
    return pl.pallas_call(
        functools.partial(_conv_lru_kernel, tt=tt, dc=dc, dl=dl, pad=pad),
        grid=(batch, nt),
        in_specs=[
            pl.BlockSpec((tt, 3 * dc), row),
            pl.BlockSpec((tt, 2 * dl), row),
            pl.BlockSpec(conv_w.shape, fixed2),
            pl.BlockSpec(lru_conv_w.shape, fixed2),
            pl.BlockSpec((1, dl), fixed2),
            pl.BlockSpec(w_a.shape, fixed3),
            pl.BlockSpec((1, dl), fixed2),
            pl.BlockSpec(w_x.shape, fixed3),
            pl.BlockSpec((1, dl), fixed2),
            pl.BlockSpec((1, dl), fixed2),
        ],
        out_specs=[pl.BlockSpec((tt, dc), row), pl.BlockSpec((tt, dl), row)],
        out_shape=[jax.ShapeDtypeStruct((n, dc), F32), jax.ShapeDtypeStruct((n, dl), F32)],
        scratch_shapes=[
            pltpu.VMEM((tt + SUBLANES, dc), F32),
            pltpu.VMEM((tt + SUBLANES, dl), F32),
            pltpu.VMEM((tt + pad, dl), F32),
            pltpu.VMEM((tt + pad, dl), F32),
            pltpu.VMEM((SUBLANES, dl), F32),
        ],
        compiler_params=_params(2),
        name="conv_lru",
    )(conv, lru, conv_w, lru_conv_w, lru_conv_b.reshape(1, dl), w_a, b_a.reshape(1, dl),
      w_x, b_x.reshape(1, dl), lam.reshape(1, dl))


def _mix_out_kernel(x_ref, yc_ref, ya_ref, yl_ref, g_ref, w_ref, *rest, n_cast):
    cast_src, o_ref, cast_dst, y_ref = rest[:n_cast], rest[n_cast], rest[n_cast + 1:2 * n_cast + 1], rest[-1]
    _cast_pieces(cast_src, cast_dst)
    col = 0
    for src in (yc_ref, ya_ref, yl_ref):
        width = src.shape[1]
        y_ref[:, col:col + width] = _rms(src[...], g_ref[:, col:col + width]).astype(BF16)
        col += width
    o_ref[...] = x_ref[...] + jnp.dot(y_ref[...], w_ref[...], preferred_element_type=F32)


def _mix_out(x, yc, ya, yl, gain, w, layer, next_w=None, tm=512):
    n, d = x.shape
    dm = w.shape[0]
    row = lambda i: (i, 0)
    in_specs = [
        pl.BlockSpec((tm, d), row),
        pl.BlockSpec((tm, yc.shape[1]), row),
        pl.BlockSpec((tm, ya.shape[1]), row),
        pl.BlockSpec((tm, yl.shape[1]), row),
        pl.BlockSpec((None, 1, dm), lambda i: (layer, 0, 0)),
        pl.BlockSpec(w.shape, lambda i: (0, 0), pipeline_mode=pl.Buffered(1)),
    ]
    args = [x, yc, ya, yl, gain.reshape(gain.shape[0], 1, dm), w]
    out_specs = [pl.BlockSpec((tm, d), row)]
    out_shape = [jax.ShapeDtypeStruct((n, d), F32)]
    if next_w is not None:
        src_spec, dst_spec, dst_shape = _row_slab_cast(next_w, layer + 1, n // tm)
        in_specs.append(src_spec)
        args.append(next_w)
        out_specs.append(dst_spec)
        out_shape.append(dst_shape)
    return pl.pallas_call(
        functools.partial(_mix_out_kernel, n_cast=0 if next_w is None else 1),
        grid=(n // tm,),
        in_specs=in_specs,
        out_specs=out_specs,
        out_shape=out_shape,
        scratch_shapes=[pltpu.VMEM((tm, dm), BF16)],
        compiler_params=_params(1),
        name="mix_out",
    )(*args)


def kernel(x, norm_ffn1, ffn1_w_in, ffn1_w_out, norm_mix, mix_w_in, conv_w, fgate_b, lru_conv_w,
           lru_conv_b, lru_w_a, lru_b_a, lru_w_x, lru_b_x, lru_lambda, mix_out_norm, mix_w_out,
           norm_ffn2, ffn2_w_in, ffn2_w_out, final_norm):
    batch, seq, d = x.shape
    depth = norm_ffn1.shape[0]
    d_conv = conv_w.shape[2]
    d_lru = lru_lambda.shape[1]
    d_att = mix_w_out.shape[1] - d_conv - d_lru
    nh = d_att // HEAD_DIM
    assert nh == fgate_b.shape[1] and DECAY_COPIES * nh <= LANES

    f0 = 3 * d_conv + 3 * d_att
    f_pad = LANES - DECAY_COPIES * nh
    fbias = jnp.pad(jnp.tile(fgate_b, (1, DECAY_COPIES)), ((0, 0), (0, f_pad)))
    w_a, w_x = lru_w_a.astype(BF16), lru_w_x.astype(BF16)
    q_scale = LOG2E * HEAD_DIM ** -0.5

    w1 = [ffn1_w_in[0].astype(BF16), ffn1_w_out[0].astype(BF16)]
    w2 = [ffn2_w_in[0].astype(BF16), ffn2_w_out[0].astype(BF16)]
    w_mix_out = mix_w_out[0].astype(BF16)
    w_mix = mix_w_in.astype(BF16)
    w_mix_lru = w_mix[:, :, f0 + nh:]
    w_mix_f = jnp.pad(jnp.tile(w_mix[:, :, f0:f0 + nh], (1, 1, DECAY_COPIES)), ((0, 0), (0, 0), (0, f_pad)))

    h = x.reshape(batch * seq, d)
    for l in range(depth):
        more = l + 1 < depth
        h, *w1_next = _ffn(h, norm_ffn1, w1[0], w1[1], l, next_w=(ffn1_w_in, ffn1_w_out) if more else None)
        conv, q, k, v, lru, cum = _mix_in(h, norm_mix, w_mix, w_mix_lru, w_mix_f, fbias, l, d_conv, d_att,
                                          q_scale, seq)
        ya = _attention(q, k, v, cum, batch, seq)
        yc, yl = _conv_lru(conv, lru, conv_w[l], lru_conv_w[l], lru_conv_b[l], w_a[l], lru_b_a[l],
                           w_x[l], lru_b_x[l], lru_lambda[l], batch, seq)
        h, *w_mix_out_next = _mix_out(h, yc, ya, yl, mix_out_norm, w_mix_out, l,
                                      next_w=mix_w_out if more else None)
        h, *w2_next = _ffn(h, norm_ffn2, w2[0], w2[1], l, next_w=(ffn2_w_in, ffn2_w_out) if more else None,
                           final_gain=None if more else final_norm)
        if more:
            w1, w2, w_mix_out = w1_next, w2_next, w_mix_out_next[0]
    return h.reshape(batch, seq, d)
```

```python
import functools

import jax
import jax.numpy as jnp
from jax import lax
from jax.experimental import pallas as pl
from jax.experimental.pallas import tpu as pltpu

F32 = jnp.float32
BF16 = jnp.bfloat16

EPS = 1e-6
LOG2E = 1.4426950408889634
HEAD_DIM = 128
LRU_BLOCKS = 4
LRU_C = 8.0
CONV_WIDTH = 3
LRU_CONV_WIDTH = 4
LANES = 128
SUBLANES = 8
VMEM_LIMIT = 58 * 1024 * 1024


def _params(n_axes, vmem=VMEM_LIMIT, **extra):
    return pltpu.CompilerParams(dimension_semantics=("arbitrary",) * n_axes,
                                vmem_limit_bytes=vmem, **extra)


def _rms(x, gain):
    ms = jnp.mean(x * x, axis=-1, keepdims=True)
    return (x * lax.rsqrt(ms + EPS)) * gain


def _cast_pieces(src_refs, dst_refs):
    for src, dst in zip(src_refs, dst_refs):
        if len(dst.shape) == 3:
            width = dst.shape[2]
            for c in range(dst.shape[0]):
                dst[c] = src[:, c * width:(c + 1) * width].astype(dst.dtype)
        else:
            dst[...] = src[...].astype(dst.dtype)


def _ffn_kernel(x_ref, g_ref, wg_ref, wu_ref, wo_ref, *rest, first_step_chunks, has_final, n_cast):
    rest = list(rest)
    gf_ref = rest.pop(0) if has_final else None
    cast_src, o_ref, cast_dst, xn_ref = rest[:n_cast], rest[n_cast], rest[n_cast + 1:2 * n_cast + 1], rest[-1]
    _cast_pieces(cast_src, cast_dst)
    k = pl.program_id(1)

    def swiglu_half(xn):
        g = jnp.dot(xn, wg_ref[...], preferred_element_type=F32)
        u = jnp.dot(xn, wu_ref[...], preferred_element_type=F32)
        h = (((g * jax.nn.sigmoid(g)) * u) * 0.5).astype(BF16)
        return jnp.dot(h, wo_ref[...], preferred_element_type=F32)

    @pl.when(k == 0)
    def _():
        rc = x_ref.shape[0] // first_step_chunks
        for c in range(first_step_chunks):
            rows = slice(c * rc, (c + 1) * rc)
            x = x_ref[rows, :]
            xn = _rms(x, g_ref[...]).astype(BF16)
            xn_ref[rows, :] = xn
            o_ref[rows, :] = x + swiglu_half(xn)

    @pl.when(k != 0)
    def _():
        o_ref[...] += swiglu_half(xn_ref[...])

    if gf_ref is not None:
        @pl.when(k == pl.num_programs(1) - 1)
        def _():
            o_ref[...] = _rms(o_ref[...], gf_ref[...])


FFN_HIDDEN_TILE = 512


def _column_blocks(w, width):
    rows, cols = w.shape
    return w.reshape(rows, cols // width, width).transpose(1, 0, 2)


def _ffn(x, gain, w_in, w_out, layer, next_w=None, final_gain=None, tm=1024, tf=FFN_HIDDEN_TILE,
         first_step_chunks=4):
    n, d = x.shape
    f = w_out.shape[0]
    nt, nk = n // tm, f // tf
    assert w_in.shape == (2 * nk, d, tf)
    in_specs = [
        pl.BlockSpec((tm, d), lambda i, k: (i, 0)),
        pl.BlockSpec((None, 1, d), lambda i, k: (layer, 0, 0)),
        pl.BlockSpec((None, d, tf), lambda i, k: (k, 0, 0)),
        pl.BlockSpec((None, d, tf), lambda i, k: (k + nk, 0, 0)),
        pl.BlockSpec((tf, d), lambda i, k: (k, 0)),
    ]
    args = [x, gain.reshape(gain.shape[0], 1, d), w_in, w_in, w_out]
    out_specs = [pl.BlockSpec((tm, d), lambda i, k: (i, 0))]
    out_shape = [jax.ShapeDtypeStruct((n, d), F32)]
    if final_gain is not None:
        in_specs.append(pl.BlockSpec((1, d), lambda i, k: (0, 0)))
        args.append(final_gain.reshape(1, d))
    if next_w is not None:
        in_tile = (d // nt, 2 * f // nk)
        out_slab = (f // (nt * nk), d)
        blocks_per_tile = in_tile[1] // tf
        in_specs += [pl.BlockSpec((None,) + in_tile, lambda i, k: (layer + 1, i, k)),
                     pl.BlockSpec((None,) + out_slab, lambda i, k: (layer + 1, i * nk + k, 0))]
        args += list(next_w)
        out_specs += [pl.BlockSpec((blocks_per_tile, in_tile[0], tf), lambda i, k: (k, i, 0)),
                      pl.BlockSpec(out_slab, lambda i, k: (i * nk + k, 0))]
        out_shape += [jax.ShapeDtypeStruct((2 * nk, d, tf), BF16), jax.ShapeDtypeStruct((f, d), BF16)]
    return pl.pallas_call(
        functools.partial(_ffn_kernel, first_step_chunks=first_step_chunks,
                          has_final=final_gain is not None, n_cast=0 if next_w is None else 2),
        grid=(nt, nk),
        in_specs=in_specs,
        out_specs=out_specs,
        out_shape=out_shape,
        scratch_shapes=[pltpu.VMEM((tm, d), BF16)],
        compiler_params=_params(2),
        name="ffn",
    )(*args)


def _mix_in_kernel(x_ref, g_ref, wm_ref, wl_ref, wf_ref, fb_ref, conv_ref, q_ref, k_ref, v_ref, lru_ref,
                   cum_ref, carry_ref, scan_ref, *, chunk, q_scale, tiles_per_seq):
    xn = _rms(x_ref[...], g_ref[...]).astype(BF16)

    def project(w_ref, col, out):
        for c in range(0, out.shape[1], chunk):
            w = min(chunk, out.shape[1] - c)
            y = jnp.dot(xn, w_ref[:, col + c:col + c + w], preferred_element_type=F32)
            if out is q_ref:
                y = y * q_scale
            out[:, c:c + w] = y.astype(out.dtype)
        return col + out.shape[1]

    col = 0
    for out in (conv_ref, q_ref, k_ref, v_ref):
        col = project(wm_ref, col, out)
    project(wl_ref, 0, lru_ref)

    @pl.when(pl.program_id(0) % tiles_per_seq == 0)
    def _():
        carry_ref[...] = jnp.zeros_like(carry_ref)

    z = jnp.dot(xn, wf_ref[...], preferred_element_type=F32) + fb_ref[...]
    t = z.shape[0]
    pad = scan_ref.shape[0] - t
    rows = slice(pad, pad + t)
    scan_ref[0:pad, :] = jnp.zeros((pad, z.shape[1]), F32)
    scan_ref[rows, :] = -(jnp.maximum(-z, 0.0) + jnp.log1p(jnp.exp(-jnp.abs(z))))
    shift = 1
    while shift < t:
        scan_ref[rows, :] = scan_ref[rows, :] + scan_ref[pad - shift:pad - shift + t, :]
        shift *= 2
    cum = scan_ref[rows, :] + carry_ref[0:1, :]
    cum_ref[...] = cum
    carry_ref[...] = jnp.broadcast_to(cum[t - 1:t, :], carry_ref.shape)


def _row_slab_cast(w_stack, layer, steps):
    slab = (w_stack.shape[1] // steps, w_stack.shape[2])
    assert slab[0] * steps == w_stack.shape[1] and slab[0] % (2 * SUBLANES) == 0
    return (pl.BlockSpec((None,) + slab, lambda i: (layer, i, 0)), pl.BlockSpec(slab, lambda i: (i, 0)),
            jax.ShapeDtypeStruct(w_stack.shape[1:], BF16))


def _mix_in(x, gain, w_all, w_lru, w_f, f_bias, layer, d_conv, d_att, q_scale, seq, tm=256, chunk=512):
    n, d = x.shape
    main = 3 * d_conv + 3 * d_att
    widths = (3 * d_conv, d_att, d_att, d_att, w_lru.shape[2], w_f.shape[2])
    dtypes = (F32, BF16, BF16, BF16, F32, F32)
    resident = lambda shape: pl.BlockSpec((None,) + shape, lambda i: (layer, 0, 0),
                                          pipeline_mode=pl.Buffered(1))
    return pl.pallas_call(
        functools.partial(_mix_in_kernel, chunk=chunk, q_scale=q_scale, tiles_per_seq=seq // tm),
        grid=(n // tm,),
        in_specs=[
            pl.BlockSpec((tm, d), lambda i: (i, 0)),
            pl.BlockSpec((None, 1, d), lambda i: (layer, 0, 0)),
            resident((d, main)), resident(w_lru.shape[1:]), resident(w_f.shape[1:]),
            pl.BlockSpec((None, 1, w_f.shape[2]), lambda i: (layer, 0, 0)),
        ],
        out_specs=[pl.BlockSpec((tm, wd), lambda i: (i, 0)) for wd in widths],
        out_shape=[jax.ShapeDtypeStruct((n, wd), dt) for wd, dt in zip(widths, dtypes)],
        scratch_shapes=[pltpu.VMEM((SUBLANES, w_f.shape[2]), F32),
                        pltpu.VMEM((tm // 2 + tm, w_f.shape[2]), F32)],
        compiler_params=_params(1),
        name="mix_in",
    )(x, gain.reshape(gain.shape[0], 1, d), w_all, w_lru, w_f, f_bias.reshape(f_bias.shape[0], 1, -1))


DECAY_COPIES = 6


def _decay_split(c_block):
    c = c_block * LOG2E
    hi = c.astype(BF16).astype(F32)
    rest = c - hi
    mid = rest.astype(BF16).astype(F32)
    return hi, mid, rest - mid


def _decay_columns(pieces, head, nh, on_query_side):
    hi, mid, lo = pieces
    lane = lax.broadcasted_iota(jnp.int32, hi.shape, 1)
    onehot = jnp.where(lane % nh == head, 1.0, 0.0)
    first = 0 if on_query_side else 3 * nh
    split = jnp.where(lane < first + nh, hi, jnp.where(lane < first + 2 * nh, mid, lo))
    if on_query_side:
        out = jnp.where(lane < 3 * nh, split, jnp.where(lane < 6 * nh, onehot, 0.0))
    else:
        out = jnp.where(lane < 3 * nh, onehot, jnp.where(lane < 6 * nh, -split, 0.0))
    return out.astype(BF16)


def _attn_kernel(q_ref, k_ref, v_ref, cq_ref, ck_ref, o_ref, kx_ref, s_ref, *, tq, nh, build_rows):
    i = pl.program_id(2)
    dh = HEAD_DIM
    hp = q_ref.shape[1] // dh
    heads = [(pl.program_id(1) * hp + g, slice(g * dh, (g + 1) * dh)) for g in range(hp)]

    @pl.when(i == 0)
    def _():
        def build(c, carry):
            rows = pl.ds(pl.multiple_of(c * build_rows, build_rows), build_rows)
            pieces = _decay_split(ck_ref[rows, :])
            for g, (head, cols) in enumerate(heads):
                kx_ref[g, rows, 0:dh] = k_ref[rows, cols]
                kx_ref[g, rows, dh:2 * dh] = _decay_columns(pieces, head, nh, on_query_side=False)
            return carry
        lax.fori_loop(0, k_ref.shape[0] // build_rows, build, 0)

    pieces = _decay_split(cq_ref[...])
    qx_t = []
    for head, cols in heads:
        qx = jnp.concatenate([q_ref[:, cols].astype(F32),
                              _decay_columns(pieces, head, nh, on_query_side=True).astype(F32)], axis=1)
        qx_t.append(qx.T.astype(BF16))

    def key_rows(j):
        return pl.ds(pl.multiple_of(j * tq, tq), tq)

    def score(j, slot):
        for g in range(hp):
            s_ref[slot, g] = jnp.dot(kx_ref[g, key_rows(j), :], qx_t[g], preferred_element_type=F32)

    def absorb(j, slot, carries, masked=False):
        probs, new = [], []
        for g in range(hp):
            m, l, acc = carries[g]
            s = s_ref[slot, g]
            if masked:
                r = lax.broadcasted_iota(jnp.int32, s.shape, 0)
                c = lax.broadcasted_iota(jnp.int32, s.shape, 1)
                s = jnp.where(r <= c, s, -jnp.inf)
            m_new = jnp.maximum(m, jnp.max(s, axis=0, keepdims=True))
            alpha = jnp.exp2(m - m_new)
            p = jnp.exp2(s - m_new)
            probs.append(p.astype(BF16))
            new.append((m_new, alpha * l + jnp.sum(p, axis=0, keepdims=True), alpha * acc))
        out = []
        for g, p in enumerate(probs):
            pv = lax.dot_general(v_ref[key_rows(j), heads[g][1]], p, (((0,), (0,)), ((), ())),
                                 preferred_element_type=F32)
            out.append((new[g][0], new[g][1], new[g][2] + pv))
        return tuple(out)

    def two_blocks(pair, carries):
        j = 2 * pair
        score(j + 1, 1)
        carries = absorb(j, 0, carries)
        score(j + 2, 0)
        return absorb(j + 1, 1, carries)

    def last_block(carries):
        return absorb(i, 0, carries, masked=True)

    def last_two_blocks(carries):
        score(i, 1)
        return absorb(i, 1, absorb(i - 1, 0, carries), masked=True)

    init = (jnp.full((1, tq), -jnp.inf, F32), jnp.zeros((1, tq), F32), jnp.zeros((dh, tq), F32))
    score(0, 0)
    carries = lax.fori_loop(0, i // 2, two_blocks, (init,) * hp)
    carries = lax.cond(i % 2 == 0, last_block, last_two_blocks, carries)
    for g, (_, cols) in enumerate(heads):
        m, l, acc = carries[g]
        o_ref[:, cols] = (acc * (1.0 / l)).T


def _attention(q, k, v, cum, batch, seq, tq=512, hp=4, build_rows=512):
    n, d_att = q.shape
    nh = d_att // HEAD_DIM
    nq = seq // tq
    width = hp * HEAD_DIM
    return pl.pallas_call(
        functools.partial(_attn_kernel, tq=tq, nh=nh, build_rows=build_rows),
        grid=(batch, nh // hp, nq),
        in_specs=[
            pl.BlockSpec((tq, width), lambda b, h, i: (b * nq + i, h)),
            pl.BlockSpec((seq, width), lambda b, h, i: (b, h)),
            pl.BlockSpec((seq, width), lambda b, h, i: (b, h)),
            pl.BlockSpec((tq, LANES), lambda b, h, i: (b * nq + i, 0)),
            pl.BlockSpec((seq, LANES), lambda b, h, i: (b, 0)),
        ],
        out_specs=pl.BlockSpec((tq, width), lambda b, h, i: (b * nq + i, h)),
        out_shape=jax.ShapeDtypeStruct((n, d_att), F32),
        scratch_shapes=[pltpu.VMEM((hp, seq, 2 * HEAD_DIM), BF16), pltpu.VMEM((2, hp, tq, tq), F32)],
        compiler_params=_params(3),
        name="fox_attention",
    )(q, k, v, cum, cum)


def _gelu_tanh(x):
    return 0.5 * x * (1.0 + jnp.tanh(0.7978845608028654 * (x + 0.044715 * (x * x * x))))


def _conv_lru_kernel(conv_ref, lru_ref, cw_ref, lw_ref, lb_ref, wa_ref, ba_ref, wx_ref, bx_ref,
                     lam_ref, yc_ref, yl_ref, zbuf, xbuf, abuf, ubuf, h_ref, *, tt, dc, dl, pad):
    @pl.when(pl.program_id(1) == 0)
    def _():
        zbuf[0:SUBLANES, :] = jnp.zeros((SUBLANES, dc), F32)
        xbuf[0:SUBLANES, :] = jnp.zeros((SUBLANES, dl), F32)
        h_ref[...] = jnp.zeros_like(h_ref)
        abuf[0:pad, :] = jnp.ones((pad, dl), F32)
        ubuf[0:pad, :] = jnp.zeros((pad, dl), F32)

    o = SUBLANES
    zbuf[o:o + tt, :] = conv_ref[:, dc:2 * dc] * conv_ref[:, 2 * dc:3 * dc]
    yc = cw_ref[CONV_WIDTH - 1:CONV_WIDTH, :] * zbuf[o:o + tt, :]
    for s in range(1, CONV_WIDTH):
        yc = yc + cw_ref[CONV_WIDTH - 1 - s:CONV_WIDTH - s, :] * zbuf[o - s:o - s + tt, :]
    yc_ref[...] = conv_ref[:, 0:dc] * yc
    zbuf[0:o, :] = zbuf[tt:tt + o, :]

    xbuf[o:o + tt, :] = lru_ref[:, dl:2 * dl]
    xr = lw_ref[LRU_CONV_WIDTH - 1:LRU_CONV_WIDTH, :] * xbuf[o:o + tt, :]
    for s in range(1, LRU_CONV_WIDTH):
        xr = xr + lw_ref[LRU_CONV_WIDTH - 1 - s:LRU_CONV_WIDTH - s, :] * xbuf[o - s:o - s + tt, :]
    xr = xr + lb_ref[...]
    xbuf[0:o, :] = xbuf[tt:tt + o, :]

    xb = xr.astype(BF16)
    blk = dl // LRU_BLOCKS
    lam = lam_ref[...]
    softplus_neg_lam = jnp.maximum(-lam, 0.0) + jnp.log1p(jnp.exp(-jnp.abs(lam)))
    for g in range(LRU_BLOCKS):
        sl = slice(g * blk, (g + 1) * blk)
        r = jax.nn.sigmoid(jnp.dot(xb[:, sl], wa_ref[g], preferred_element_type=F32) + ba_ref[:, sl])
        ig = jax.nn.sigmoid(jnp.dot(xb[:, sl], wx_ref[g], preferred_element_type=F32) + bx_ref[:, sl])
        log_a = (-LRU_C * r) * softplus_neg_lam[:, sl]
        a = jnp.exp(log_a)
        th = jnp.tanh(log_a)
        mult = jnp.sqrt(-2.0 * th / (1.0 - th))
        abuf[pad:pad + tt, sl] = a
        ubuf[pad:pad + tt, sl] = mult * (ig * xr[:, sl])

    s = 1
    while s < tt:
        a = abuf[pad:pad + tt, :]
        u = ubuf[pad:pad + tt, :]
        a_sh = abuf[pad - s:pad - s + tt, :]
        u_sh = ubuf[pad - s:pad - s + tt, :]
        ubuf[pad:pad + tt, :] = a * u_sh + u
        abuf[pad:pad + tt, :] = a * a_sh
        s *= 2
    hs = abuf[pad:pad + tt, :] * h_ref[0:1, :] + ubuf[pad:pad + tt, :]
    h_ref[...] = jnp.broadcast_to(hs[tt - 1:tt, :], h_ref.shape)
    yl_ref[...] = _gelu_tanh(lru_ref[:, 0:dl]) * hs


def _conv_lru(conv, lru, conv_w, lru_conv_w, lru_conv_b, w_a, b_a, w_x, b_x, lam, batch, seq, tt=256):
    n = conv.shape[0]
    dc = conv.shape[1] // 3
    dl = lru.shape[1] // 2
    nt = seq // tt
    pad = tt // 2
    row = lambda b, t: (b * nt + t, 0)
    fixed2 = lambda b, t: (0, 0)
    fixed3 = lambda b, t: (0, 0, 0)
    return pl.pallas_call(
        functools.partial(_conv_lru_kernel, tt=tt, dc=dc, dl=dl, pad=pad),
        grid=(batch, nt),
        in_specs=[
            pl.BlockSpec((tt, 3 * dc), row),
            pl.BlockSpec((tt, 2 * dl), row),
            pl.BlockSpec(conv_w.shape, fixed2),
            pl.BlockSpec(lru_conv_w.shape, fixed2),
            pl.BlockSpec((1, dl), fixed2),
            pl.BlockSpec(w_a.shape, fixed3),
            pl.BlockSpec((1, dl), fixed2),
            pl.BlockSpec(w_x.shape, fixed3),
            pl.BlockSpec((1, dl), fixed2),
            pl.BlockSpec((1, dl), fixed2),
        ],
        out_specs=[pl.BlockSpec((tt, dc), row), pl.BlockSpec((tt, dl), row)],
        out_shape=[jax.ShapeDtypeStruct((n, dc), F32), jax.ShapeDtypeStruct((n, dl), F32)],
        scratch_shapes=[
            pltpu.VMEM((tt + SUBLANES, dc), F32),
            pltpu.VMEM((tt + SUBLANES, dl), F32),
            pltpu.VMEM((tt + pad, dl), F32),
            pltpu.VMEM((tt + pad, dl), F32),
            pltpu.VMEM((SUBLANES, dl), F32),
        ],
        compiler_params=_params(2),
        name="conv_lru",
    )(conv, lru, conv_w, lru_conv_w, lru_conv_b.reshape(1, dl), w_a, b_a.reshape(1, dl),
      w_x, b_x.reshape(1, dl), lam.reshape(1, dl))


def _mix_out_kernel(x_ref, yc_ref, ya_ref, yl_ref, g_ref, w_ref, *rest, n_cast):
    cast_src, o_ref, cast_dst, y_ref = rest[:n_cast], rest[n_cast], rest[n_cast + 1:2 * n_cast + 1], rest[-1]
    _cast_pieces(cast_src, cast_dst)
    col = 0
    for src in (yc_ref, ya_ref, yl_ref):
        width = src.shape[1]
        y_ref[:, col:col + width] = _rms(src[...], g_ref[:, col:col + width]).astype(BF16)
        col += width
    o_ref[...] = x_ref[...] + jnp.dot(y_ref[...], w_ref[...], preferred_element_type=F32)


def _mix_out(x, yc, ya, yl, gain, w, layer, next_w=None, tm=512):
    n, d = x.shape
    dm = w.shape[0]
    row = lambda i: (i, 0)
    in_specs = [
        pl.BlockSpec((tm, d), row),
        pl.BlockSpec((tm, yc.shape[1]), row),
        pl.BlockSpec((tm, ya.shape[1]), row),
        pl.BlockSpec((tm, yl.shape[1]), row),
        pl.BlockSpec((None, 1, dm), lambda i: (layer, 0, 0)),
        pl.BlockSpec(w.shape, lambda i: (0, 0), pipeline_mode=pl.Buffered(1)),
    ]
    args = [x, yc, ya, yl, gain.reshape(gain.shape[0], 1, dm), w]
    out_specs = [pl.BlockSpec((tm, d), row)]
    out_shape = [jax.ShapeDtypeStruct((n, d), F32)]
    if next_w is not None:
        src_spec, dst_spec, dst_shape = _row_slab_cast(next_w, layer + 1, n // tm)
        in_specs.append(src_spec)
        args.append(next_w)
        out_specs.append(dst_spec)
        out_shape.append(dst_shape)
    return pl.pallas_call(
        functools.partial(_mix_out_kernel, n_cast=0 if next_w is None else 1),
        grid=(n // tm,),
        in_specs=in_specs,
        out_specs=out_specs,
        out_shape=out_shape,
        scratch_shapes=[pltpu.VMEM((tm, dm), BF16)],
        compiler_params=_params(1),
        name="mix_out",
    )(*args)


def kernel(x, norm_ffn1, ffn1_w_in, ffn1_w_out, norm_mix, mix_w_in, conv_w, fgate_b, lru_conv_w,
           lru_conv_b, lru_w_a, lru_b_a, lru_w_x, lru_b_x, lru_lambda, mix_out_norm, mix_w_out,
           norm_ffn2, ffn2_w_in, ffn2_w_out, final_norm):
    batch, seq, d = x.shape
    depth = norm_ffn1.shape[0]
    d_conv = conv_w.shape[2]
    d_lru = lru_lambda.shape[1]
    d_att = mix_w_out.shape[1] - d_conv - d_lru
    nh = d_att // HEAD_DIM
    assert nh == fgate_b.shape[1] and DECAY_COPIES * nh <= LANES

    f0 = 3 * d_conv + 3 * d_att
    f_pad = LANES - DECAY_COPIES * nh
    fbias = jnp.pad(jnp.tile(fgate_b, (1, DECAY_COPIES)), ((0, 0), (0, f_pad)))
    w_a, w_x = lru_w_a.astype(BF16), lru_w_x.astype(BF16)
    q_scale = LOG2E * HEAD_DIM ** -0.5

    w1 = [_column_blocks(ffn1_w_in[0].astype(BF16), FFN_HIDDEN_TILE), ffn1_w_out[0].astype(BF16)]
    w2 = [_column_blocks(ffn2_w_in[0].astype(BF16), FFN_HIDDEN_TILE), ffn2_w_out[0].astype(BF16)]
    w_mix_out = mix_w_out[0].astype(BF16)
    w_mix = mix_w_in.astype(BF16)
    w_mix_lru = w_mix[:, :, f0 + nh:]
    w_mix_f = jnp.pad(jnp.tile(w_mix[:, :, f0:f0 + nh], (1, 1, DECAY_COPIES)), ((0, 0), (0, 0), (0, f_pad)))

    h = x.reshape(batch * seq, d)
    for l in range(depth):
        more = l + 1 < depth
        h, *w1_next = _ffn(h, norm_ffn1, w1[0], w1[1], l, next_w=(ffn1_w_in, ffn1_w_out) if more else None)
        conv, q, k, v, lru, cum = _mix_in(h, norm_mix, w_mix, w_mix_lru, w_mix_f, fbias, l, d_conv, d_att,
                                          q_scale, seq)
        ya = _attention(q, k, v, cum, batch, seq)
        yc, yl = _conv_lru(conv, lru, conv_w[l], lru_conv_w[l], lru_conv_b[l], w_a[l], lru_b_a[l],
                           w_x[l], lru_b_x[l], lru_lambda[l], batch, seq)
        h, *w_mix_out_next = _mix_out(h, yc, ya, yl, mix_out_norm, w_mix_out, l,
                                      next_w=mix_w_out if more else None)
        h, *w2_next = _ffn(h, norm_ffn2, w2[0], w2[1], l, next_w=(ffn2_w_in, ffn2_w_out) if more else None,
                           final_gain=None if more else final_norm)
        if more:
            w1, w2, w_mix_out = w1_next, w2_next, w_mix_out_next[0]
    return h.reshape(batch, seq, d)
```

```python
import functools

import jax
import jax.numpy as jnp
from jax import lax
from jax.experimental import pallas as pl
from jax.experimental.pallas import tpu as pltpu

F32 = jnp.float32
BF16 = jnp.bfloat16

EPS = 1e-6
LOG2E = 1.4426950408889634
HEAD_DIM = 128
LRU_BLOCKS = 4
LRU_C = 8.0
CONV_WIDTH = 3
LRU_CONV_WIDTH = 4
LANES = 128
SUBLANES = 8
VMEM_LIMIT = 58 * 1024 * 1024


def _params(n_axes, vmem=VMEM_LIMIT, **extra):
    return pltpu.CompilerParams(dimension_semantics=("arbitrary",) * n_axes,
                                vmem_limit_bytes=vmem, **extra)


def _rms(x, gain):
    ms = jnp.mean(x * x, axis=-1, keepdims=True)
    return (x * lax.rsqrt(ms + EPS)) * gain


def _cast_pieces(src_refs, dst_refs):
    for src, dst in zip(src_refs, dst_refs):
        dst[...] = src[...].astype(dst.dtype)


def _ffn_kernel(x_ref, g_ref, wg_ref, wu_ref, wo_ref, *rest, first_step_chunks, has_final, n_cast):
    rest = list(rest)
    gf_ref = rest.pop(0) if has_final else None
    cast_src, o_ref, cast_dst, xn_ref = rest[:n_cast], rest[n_cast], rest[n_cast + 1:2 * n_cast + 1], rest[-1]
    _cast_pieces(cast_src, cast_dst)
    k = pl.program_id(1)

    def swiglu_half(xn):
        g = jnp.dot(xn, wg_ref[...], preferred_element_type=F32)
        u = jnp.dot(xn, wu_ref[...], preferred_element_type=F32)
        h = (((g * jax.nn.sigmoid(g)) * u) * 0.5).astype(BF16)
        return jnp.dot(h, wo_ref[...], preferred_element_type=F32)

    @pl.when(k == 0)
    def _():
        rc = x_ref.shape[0] // first_step_chunks
        for c in range(first_step_chunks):
            rows = slice(c * rc, (c + 1) * rc)
            x = x_ref[rows, :]
            xn = _rms(x, g_ref[...]).astype(BF16)
            xn_ref[rows, :] = xn
            o_ref[rows, :] = x + swiglu_half(xn)

    @pl.when(k != 0)
    def _():
        o_ref[...] += swiglu_half(xn_ref[...])

    if gf_ref is not None:
        @pl.when(k == pl.num_programs(1) - 1)
        def _():
            o_ref[...] = _rms(o_ref[...], gf_ref[...])


def _ffn(x, gain, w_in, w_out, layer, next_w=None, final_gain=None, tm=1024, tf=512, first_step_chunks=4):
    n, d = x.shape
    f = w_out.shape[0]
    nt, nk = n // tm, f // tf
    in_specs = [
        pl.BlockSpec((tm, d), lambda i, k: (i, 0)),
        pl.BlockSpec((None, 1, d), lambda i, k: (layer, 0, 0)),
        pl.BlockSpec((d, tf), lambda i, k: (0, k)),
        pl.BlockSpec((d, tf), lambda i, k: (0, k + nk)),
        pl.BlockSpec((tf, d), lambda i, k: (k, 0)),
    ]
    args = [x, gain.reshape(gain.shape[0], 1, d), w_in, w_in, w_out]
    out_specs = [pl.BlockSpec((tm, d), lambda i, k: (i, 0))]
    out_shape = [jax.ShapeDtypeStruct((n, d), F32)]
    if final_gain is not None:
        in_specs.append(pl.BlockSpec((1, d), lambda i, k: (0, 0)))
        args.append(final_gain.reshape(1, d))
    if next_w is not None:
        in_tile = (d // nt, 2 * f // nk)
        out_slab = (f // (nt * nk), d)
        in_specs += [pl.BlockSpec((None,) + in_tile, lambda i, k: (layer + 1, i, k)),
                     pl.BlockSpec((None,) + out_slab, lambda i, k: (layer + 1, i * nk + k, 0))]
        args += list(next_w)
        out_specs += [pl.BlockSpec(in_tile, lambda i, k: (i, k)),
                      pl.BlockSpec(out_slab, lambda i, k: (i * nk + k, 0))]
        out_shape += [jax.ShapeDtypeStruct((d, 2 * f), BF16), jax.ShapeDtypeStruct((f, d), BF16)]
    return pl.pallas_call(
        functools.partial(_ffn_kernel, first_step_chunks=first_step_chunks,
                          has_final=final_gain is not None, n_cast=0 if next_w is None else 2),
        grid=(nt, nk),
        in_specs=in_specs,
        out_specs=out_specs,
        out_shape=out_shape,
        scratch_shapes=[pltpu.VMEM((tm, d), BF16)],
        compiler_params=_params(2),
        name="ffn",
    )(*args)


def _gelu_tanh(x):
    return 0.5 * x * (1.0 + jnp.tanh(0.7978845608028654 * (x + 0.044715 * (x * x * x))))


def _mix_in_kernel(x_ref, g_ref, wm_ref, wl_ref, wf_ref, fb_ref, cw_ref, lw_ref, lb_ref, wa_ref, ba_ref,
                   wx_ref, bx_ref, lam_ref, yc_ref, q_ref, k_ref, v_ref, yl_ref, cum_ref,
                   carry_ref, scan_ref, cb_ref, lg_ref, zbuf, xbuf, abuf, ubuf, h_ref,
                   *, chunk, q_scale, tiles_per_seq):
    tt, dc, dl = x_ref.shape[0], yc_ref.shape[1], yl_ref.shape[1]
    o = SUBLANES
    pad = abuf.shape[0] - tt

    @pl.when(pl.program_id(0) % tiles_per_seq == 0)
    def _():
        carry_ref[...] = jnp.zeros_like(carry_ref)
        zbuf[0:o, :] = jnp.zeros((o, dc), F32)
        xbuf[0:o, :] = jnp.zeros((o, dl), F32)
        h_ref[...] = jnp.zeros_like(h_ref)
        abuf[0:pad, :] = jnp.ones((pad, dl), F32)
        ubuf[0:pad, :] = jnp.zeros((pad, dl), F32)

    xn = _rms(x_ref[...], g_ref[...]).astype(BF16)

    def proj(w_ref, col, width):
        return jnp.dot(xn, w_ref[:, col:col + width], preferred_element_type=F32)

    def conv3():
        yc = cw_ref[CONV_WIDTH - 1:CONV_WIDTH, :] * zbuf[o:o + tt, :]
        for s in range(1, CONV_WIDTH):
            yc = yc + cw_ref[CONV_WIDTH - 1 - s:CONV_WIDTH - s, :] * zbuf[o - s:o - s + tt, :]
        yc_ref[...] = cb_ref[...] * yc
        zbuf[0:o, :] = zbuf[tt:tt + o, :]

    def conv4():
        xr = lw_ref[LRU_CONV_WIDTH - 1:LRU_CONV_WIDTH, :] * xbuf[o:o + tt, :]
        for s in range(1, LRU_CONV_WIDTH):
            xr = xr + lw_ref[LRU_CONV_WIDTH - 1 - s:LRU_CONV_WIDTH - s, :] * xbuf[o - s:o - s + tt, :]
        xr = xr + lb_ref[...]
        xbuf[0:o, :] = xbuf[tt:tt + o, :]
        xbuf[o:o + tt, :] = xr

    blk = dl // LRU_BLOCKS
    lam = lam_ref[...]
    softplus_neg_lam = jnp.maximum(-lam, 0.0) + jnp.log1p(jnp.exp(-jnp.abs(lam)))

    def lru_inputs(g):
        sl = slice(g * blk, (g + 1) * blk)
        xg = xbuf[o:o + tt, sl]
        xg_bf = xg.astype(BF16)
        r = jax.nn.sigmoid(jnp.dot(xg_bf, wa_ref[g], preferred_element_type=F32) + ba_ref[:, sl])
        ig = jax.nn.sigmoid(jnp.dot(xg_bf, wx_ref[g], preferred_element_type=F32) + bx_ref[:, sl])
        log_a = (-LRU_C * r) * softplus_neg_lam[:, sl]
        th = jnp.tanh(log_a)
        mult = jnp.sqrt(-2.0 * th / (1.0 - th))
        abuf[pad:pad + tt, sl] = jnp.exp(log_a)
        ubuf[pad:pad + tt, sl] = mult * (ig * xg)

    def scan_step(s):
        a = abuf[pad:pad + tt, :]
        u = ubuf[pad:pad + tt, :]
        a_sh = abuf[pad - s:pad - s + tt, :]
        u_sh = ubuf[pad - s:pad - s + tt, :]
        ubuf[pad:pad + tt, :] = a * u_sh + u
        abuf[pad:pad + tt, :] = a * a_sh

    def finish_lru():
        hs = abuf[pad:pad + tt, :] * h_ref[0:1, :] + ubuf[pad:pad + tt, :]
        h_ref[...] = jnp.broadcast_to(hs[tt - 1:tt, :], h_ref.shape)
        yl_ref[...] = _gelu_tanh(lg_ref[...]) * hs

    def store(out, c, w, scale=None):
        def run(col):
            y = proj(wm_ref, col, w)
            out[:, c:c + w] = (y if scale is None else y * scale).astype(out.dtype)
        return run

    def store_cb(col):
        cb_ref[...] = proj(wm_ref, col, dc)

    def store_lg(col):
        lg_ref[...] = proj(wl_ref, 0, dl)

    zbuf[o:o + tt, :] = proj(wm_ref, dc, dc) * proj(wm_ref, 2 * dc, dc)
    xbuf[o:o + tt, :] = proj(wl_ref, dl, dl)
    mxu_work = [(store_cb, 0), (store_lg, 0)]
    col = 3 * dc
    for out, scale in ((q_ref, q_scale), (k_ref, None), (v_ref, None)):
        for c in range(0, out.shape[1], chunk):
            w = min(chunk, out.shape[1] - c)
            mxu_work.append((store(out, c, w, scale), col + c))
        col += out.shape[1]
    scans = []
    s = 1
    while s < tt:
        scans.append(functools.partial(scan_step, s))
        s *= 2
    vpu_work = ([[conv4], [functools.partial(lru_inputs, 0)], [conv3]]
                + [[functools.partial(lru_inputs, g)] for g in range(1, LRU_BLOCKS)]
                + [scans[:len(scans) // 2], scans[len(scans) // 2:] + [finish_lru]])
    assert len(mxu_work) >= len(vpu_work)
    for n, (dot, wcol) in enumerate(mxu_work):
        dot(wcol)
        for piece in (vpu_work[n] if n < len(vpu_work) else []):
            piece()

    z = jnp.dot(xn, wf_ref[...], preferred_element_type=F32) + fb_ref[...]
    top = scan_ref.shape[0] - tt
    rows = slice(top, top + tt)
    scan_ref[0:top, :] = jnp.zeros((top, z.shape[1]), F32)
    scan_ref[rows, :] = -(jnp.maximum(-z, 0.0) + jnp.log1p(jnp.exp(-jnp.abs(z))))
    shift = 1
    while shift < tt:
        scan_ref[rows, :] = scan_ref[rows, :] + scan_ref[top - shift:top - shift + tt, :]
        shift *= 2
    cum = scan_ref[rows, :] + carry_ref[0:1, :]
    cum_ref[...] = cum
    carry_ref[...] = jnp.broadcast_to(cum[tt - 1:tt, :], carry_ref.shape)


def _row_slab_cast(w_stack, layer, steps):
    slab = (w_stack.shape[1] // steps, w_stack.shape[2])
    assert slab[0] * steps == w_stack.shape[1] and slab[0] % (2 * SUBLANES) == 0
    return (pl.BlockSpec((None,) + slab, lambda i: (layer, i, 0)), pl.BlockSpec(slab, lambda i: (i, 0)),
            jax.ShapeDtypeStruct(w_stack.shape[1:], BF16))


def _mix_in(x, gain, w_all, w_lru, w_f, f_bias, conv_w, lru_conv_w, lru_conv_b, w_a, b_a, w_x, b_x, lam,
            layer, d_att, q_scale, seq, tm=256, chunk=512):
    n, d = x.shape
    d_conv, d_lru, f_lanes = conv_w.shape[2], lam.shape[1], w_f.shape[2]
    main = 3 * d_conv + 3 * d_att
    widths = (d_conv, d_att, d_att, d_att, d_lru, f_lanes)
    dtypes = (F32, BF16, BF16, BF16, F32, F32)
    pad = tm // 2
    per_layer = lambda a: pl.BlockSpec((None,) + a.shape[1:], lambda i: (layer,) + (0,) * (a.ndim - 1))
    resident = lambda shape: pl.BlockSpec((None,) + shape, lambda i: (layer, 0, 0),
                                          pipeline_mode=pl.Buffered(1))
    rows3 = lambda a: a.reshape(a.shape[0], 1, a.shape[1])
    small = [conv_w, lru_conv_w, rows3(lru_conv_b), w_a, rows3(b_a), w_x, rows3(b_x), rows3(lam)]
    return pl.pallas_call(
        functools.partial(_mix_in_kernel, chunk=chunk, q_scale=q_scale, tiles_per_seq=seq // tm),
        grid=(n // tm,),
        in_specs=[
            pl.BlockSpec((tm, d), lambda i: (i, 0)),
            pl.BlockSpec((None, 1, d), lambda i: (layer, 0, 0)),
            resident((d, main)), resident(w_lru.shape[1:]), resident(w_f.shape[1:]),
            pl.BlockSpec((None, 1, f_lanes), lambda i: (layer, 0, 0)),
        ] + [per_layer(a) for a in small],
        out_specs=[pl.BlockSpec((tm, wd), lambda i: (i, 0)) for wd in widths],
        out_shape=[jax.ShapeDtypeStruct((n, wd), dt) for wd, dt in zip(widths, dtypes)],
        scratch_shapes=[
            pltpu.VMEM((SUBLANES, f_lanes), F32),
            pltpu.VMEM((pad + tm, f_lanes), F32),
            pltpu.VMEM((tm, d_conv), F32),
            pltpu.VMEM((tm, d_lru), F32),
            pltpu.VMEM((tm + SUBLANES, d_conv), F32),
            pltpu.VMEM((tm + SUBLANES, d_lru), F32),
            pltpu.VMEM((pad + tm, d_lru), F32),
            pltpu.VMEM((pad + tm, d_lru), F32),
            pltpu.VMEM((SUBLANES, d_lru), F32),
        ],
        compiler_params=_params(1),
        name="mix_in",
    )(x, gain.reshape(gain.shape[0], 1, d), w_all, w_lru, w_f, rows3(f_bias), *small)


DECAY_COPIES = 6


def _decay_split(c_block):
    c = c_block * LOG2E
    hi = c.astype(BF16).astype(F32)
    rest = c - hi
    mid = rest.astype(BF16).astype(F32)
    return hi, mid, rest - mid


def _decay_columns(pieces, head, nh, on_query_side):
    hi, mid, lo = pieces
    lane = lax.broadcasted_iota(jnp.int32, hi.shape, 1)
    onehot = jnp.where(lane % nh == head, 1.0, 0.0)
    first = 0 if on_query_side else 3 * nh
    split = jnp.where(lane < first + nh, hi, jnp.where(lane < first + 2 * nh, mid, lo))
    if on_query_side:
        out = jnp.where(lane < 3 * nh, split, jnp.where(lane < 6 * nh, onehot, 0.0))
    else:
        out = jnp.where(lane < 3 * nh, onehot, jnp.where(lane < 6 * nh, -split, 0.0))
    return out.astype(BF16)


def _attn_kernel(q_ref, k_ref, v_ref, cq_ref, ck_ref, o_ref, kx_ref, s_ref, *, tq, nh, build_rows):
    i = pl.program_id(2)
    dh = HEAD_DIM
    hp = q_ref.shape[1] // dh
    heads = [(pl.program_id(1) * hp + g, slice(g * dh, (g + 1) * dh)) for g in range(hp)]

    @pl.when(i == 0)
    def _():
        def build(c, carry):
            rows = pl.ds(pl.multiple_of(c * build_rows, build_rows), build_rows)
            pieces = _decay_split(ck_ref[rows, :])
            for g, (head, cols) in enumerate(heads):
                kx_ref[g, rows, 0:dh] = k_ref[rows, cols]
                kx_ref[g, rows, dh:2 * dh] = _decay_columns(pieces, head, nh, on_query_side=False)
            return carry
        lax.fori_loop(0, k_ref.shape[0] // build_rows, build, 0)

    pieces = _decay_split(cq_ref[...])
    qx_t = []
    for head, cols in heads:
        qx = jnp.concatenate([q_ref[:, cols].astype(F32),
                              _decay_columns(pieces, head, nh, on_query_side=True).astype(F32)], axis=1)
        qx_t.append(qx.T.astype(BF16))

    def key_rows(j):
        return pl.ds(pl.multiple_of(j * tq, tq), tq)

    def score(j, slot):
        for g in range(hp):
            s_ref[slot, g] = jnp.dot(kx_ref[g, key_rows(j), :], qx_t[g], preferred_element_type=F32)

    def absorb(j, slot, carries, masked=False):
        probs, new = [], []
        for g in range(hp):
            m, l, acc = carries[g]
            s = s_ref[slot, g]
            if masked:
                r = lax.broadcasted_iota(jnp.int32, s.shape, 0)
                c = lax.broadcasted_iota(jnp.int32, s.shape, 1)
                s = jnp.where(r <= c, s, -jnp.inf)
            m_new = jnp.maximum(m, jnp.max(s, axis=0, keepdims=True))
            alpha = jnp.exp2(m - m_new)
            p = jnp.exp2(s - m_new)
            probs.append(p.astype(BF16))
            new.append((m_new, alpha * l + jnp.sum(p, axis=0, keepdims=True), alpha * acc))
        out = []
        for g, p in enumerate(probs):
            pv = lax.dot_general(v_ref[key_rows(j), heads[g][1]], p, (((0,), (0,)), ((), ())),
                                 preferred_element_type=F32)
            out.append((new[g][0], new[g][1], new[g][2] + pv))
        return tuple(out)

    def two_blocks(pair, carries):
        j = 2 * pair
        score(j + 1, 1)
        carries = absorb(j, 0, carries)
        score(j + 2, 0)
        return absorb(j + 1, 1, carries)

    def last_block(carries):
        return absorb(i, 0, carries, masked=True)

    def last_two_blocks(carries):
        score(i, 1)
        return absorb(i, 1, absorb(i - 1, 0, carries), masked=True)

    init = (jnp.full((1, tq), -jnp.inf, F32), jnp.zeros((1, tq), F32), jnp.zeros((dh, tq), F32))
    score(0, 0)
    carries = lax.fori_loop(0, i // 2, two_blocks, (init,) * hp)
    carries = lax.cond(i % 2 == 0, last_block, last_two_blocks, carries)
    for g, (_, cols) in enumerate(heads):
        m, l, acc = carries[g]
        o_ref[:, cols] = (acc * (1.0 / l)).T


def _attention(q, k, v, cum, batch, seq, tq=512, hp=4, build_rows=512):
    n, d_att = q.shape
    nh = d_att // HEAD_DIM
    nq = seq // tq
    width = hp * HEAD_DIM
    return pl.pallas_call(
        functools.partial(_attn_kernel, tq=tq, nh=nh, build_rows=build_rows),
        grid=(batch, nh // hp, nq),
        in_specs=[
            pl.BlockSpec((tq, width), lambda b, h, i: (b * nq + i, h)),
            pl.BlockSpec((seq, width), lambda b, h, i: (b, h)),
            pl.BlockSpec((seq, width), lambda b, h, i: (b, h)),
            pl.BlockSpec((tq, LANES), lambda b, h, i: (b * nq + i, 0)),
            pl.BlockSpec((seq, LANES), lambda b, h, i: (b, 0)),
        ],
        out_specs=pl.BlockSpec((tq, width), lambda b, h, i: (b * nq + i, h)),
        out_shape=jax.ShapeDtypeStruct((n, d_att), F32),
        scratch_shapes=[pltpu.VMEM((hp, seq, 2 * HEAD_DIM), BF16), pltpu.VMEM((2, hp, tq, tq), F32)],
        compiler_params=_params(3),
        name="fox_attention",
    )(q, k, v, cum, cum)


def _mix_out_kernel(x_ref, yc_ref, ya_ref, yl_ref, g_ref, w_ref, *rest, n_cast):
    cast_src, o_ref, cast_dst, y_ref = rest[:n_cast], rest[n_cast], rest[n_cast + 1:2 * n_cast + 1], rest[-1]
    _cast_pieces(cast_src, cast_dst)
    col = 0
    for src in (yc_ref, ya_ref, yl_ref):
        width = src.shape[1]
        y_ref[:, col:col + width] = _rms(src[...], g_ref[:, col:col + width]).astype(BF16)
        col += width
    o_ref[...] = x_ref[...] + jnp.dot(y_ref[...], w_ref[...], preferred_element_type=F32)


def _mix_out(x, yc, ya, yl, gain, w, layer, next_w=None, tm=512):
    n, d = x.shape
    dm = w.shape[0]
    row = lambda i: (i, 0)
    in_specs = [
        pl.BlockSpec((tm, d), row),
        pl.BlockSpec((tm, yc.shape[1]), row),
        pl.BlockSpec((tm, ya.shape[1]), row),
        pl.BlockSpec((tm, yl.shape[1]), row),
        pl.BlockSpec((None, 1, dm), lambda i: (layer, 0, 0)),
        pl.BlockSpec(w.shape, lambda i: (0, 0), pipeline_mode=pl.Buffered(1)),
    ]
    args = [x, yc, ya, yl, gain.reshape(gain.shape[0], 1, dm), w]
    out_specs = [pl.BlockSpec((tm, d), row)]
    out_shape = [jax.ShapeDtypeStruct((n, d), F32)]
    if next_w is not None:
        src_spec, dst_spec, dst_shape = _row_slab_cast(next_w, layer + 1, n // tm)
        in_specs.append(src_spec)
        args.append(next_w)
        out_specs.append(dst_spec)
        out_shape.append(dst_shape)
    return pl.pallas_call(
        functools.partial(_mix_out_kernel, n_cast=0 if next_w is None else 1),
        grid=(n // tm,),
        in_specs=in_specs,
        out_specs=out_specs,
        out_shape=out_shape,
        scratch_shapes=[pltpu.VMEM((tm, dm), BF16)],
        compiler_params=_params(1),
        name="mix_out",
    )(*args)


def kernel(x, norm_ffn1, ffn1_w_in, ffn1_w_out, norm_mix, mix_w_in, conv_w, fgate_b, lru_conv_w,
           lru_conv_b, lru_w_a, lru_b_a, lru_w_x, lru_b_x, lru_lambda, mix_out_norm, mix_w_out,
           norm_ffn2, ffn2_w_in, ffn2_w_out, final_norm):
    batch, seq, d = x.shape
    depth = norm_ffn1.shape[0]
    d_conv = conv_w.shape[2]
    d_lru = lru_lambda.shape[1]
    d_att = mix_w_out.shape[1] - d_conv - d_lru
    nh = d_att // HEAD_DIM
    assert nh == fgate_b.shape[1] and DECAY_COPIES * nh <= LANES

    f0 = 3 * d_conv + 3 * d_att
    f_pad = LANES - DECAY_COPIES * nh
    fbias = jnp.pad(jnp.tile(fgate_b, (1, DECAY_COPIES)), ((0, 0), (0, f_pad)))
    w_a, w_x = lru_w_a.astype(BF16), lru_w_x.astype(BF16)
    q_scale = LOG2E * HEAD_DIM ** -0.5

    w1 = [ffn1_w_in[0].astype(BF16), ffn1_w_out[0].astype(BF16)]
    w2 = [ffn2_w_in[0].astype(BF16), ffn2_w_out[0].astype(BF16)]
    w_mix_out = mix_w_out[0].astype(BF16)
    w_mix = mix_w_in.astype(BF16)
    w_mix_lru = w_mix[:, :, f0 + nh:]
    w_mix_f = jnp.pad(jnp.tile(w_mix[:, :, f0:f0 + nh], (1, 1, DECAY_COPIES)), ((0, 0), (0, 0), (0, f_pad)))

    h = x.reshape(batch * seq, d)
    for l in range(depth):
        more = l + 1 < depth
        h, *w1_next = _ffn(h, norm_ffn1, w1[0], w1[1], l, next_w=(ffn1_w_in, ffn1_w_out) if more else None)
        yc, q, k, v, yl, cum = _mix_in(h, norm_mix, w_mix, w_mix_lru, w_mix_f, fbias, conv_w, lru_conv_w,
                                       lru_conv_b, w_a, lru_b_a, w_x, lru_b_x, lru_lambda, l, d_att,
                                       q_scale, seq)
        ya = _attention(q, k, v, cum, batch, seq)
        h, *w_mix_out_next = _mix_out(h, yc, ya, yl, mix_out_norm, w_mix_out, l,
                                      next_w=mix_w_out if more else None)
        h, *w2_next = _ffn(h, norm_ffn2, w2[0], w2[1], l, next_w=(ffn2_w_in, ffn2_w_out) if more else None,
                           final_gain=None if more else final_norm)
        if more:
            w1, w2, w_mix_out = w1_next, w2_next, w_mix_out_next[0]
    return h.reshape(batch, seq, d)
```

```python
import functools

import jax
import jax.numpy as jnp
from jax import lax
from jax.experimental import pallas as pl
from jax.experimental.pallas import tpu as pltpu

F32 = jnp.float32
BF16 = jnp.bfloat16

EPS = 1e-6
LOG2E = 1.4426950408889634
HEAD_DIM = 128
LRU_BLOCKS = 4
LRU_C = 8.0
CONV_WIDTH = 3
LRU_CONV_WIDTH = 4
LANES = 128
SUBLANES = 8
VMEM_LIMIT = 58 * 1024 * 1024


def _params(n_axes, vmem=VMEM_LIMIT, **extra):
    return pltpu.CompilerParams(dimension_semantics=("arbitrary",) * n_axes,
                                vmem_limit_bytes=vmem, **extra)


def _rms(x, gain):
    ms = jnp.mean(x * x, axis=-1, keepdims=True)
    return (x * lax.rsqrt(ms + EPS)) * gain


def _cast_pieces(src_refs, dst_refs):
    for src, dst in zip(src_refs, dst_refs):
        dst[...] = src[...].astype(dst.dtype)


def _ffn_kernel(x_ref, g_ref, wg_ref, wu_ref, wo_ref, *rest, first_step_chunks, has_final, n_cast):
    rest = list(rest)
    gf_ref = rest.pop(0) if has_final else None
    cast_src, o_ref, cast_dst, xn_ref = rest[:n_cast], rest[n_cast], rest[n_cast + 1:2 * n_cast + 1], rest[-1]
    _cast_pieces(cast_src, cast_dst)
    k = pl.program_id(1)

    def swiglu_half(xn):
        g = jnp.dot(xn, wg_ref[...], preferred_element_type=F32)
        u = jnp.dot(xn, wu_ref[...], preferred_element_type=F32)
        h = (((g * jax.nn.sigmoid(g)) * u) * 0.5).astype(BF16)
        return jnp.dot(h, wo_ref[...], preferred_element_type=F32)

    @pl.when(k == 0)
    def _():
        rc = x_ref.shape[0] // first_step_chunks
        for c in range(first_step_chunks):
            rows = slice(c * rc, (c + 1) * rc)
            x = x_ref[rows, :]
            xn = _rms(x, g_ref[...]).astype(BF16)
            xn_ref[rows, :] = xn
            o_ref[rows, :] = x + swiglu_half(xn)

    @pl.when(k != 0)
    def _():
        o_ref[...] += swiglu_half(xn_ref[...])

    if gf_ref is not None:
        @pl.when(k == pl.num_programs(1) - 1)
        def _():
            o_ref[...] = _rms(o_ref[...], gf_ref[...])


def _ffn(x, gain, w_in, w_out, layer, next_w_in=None, final_gain=None, tm=1024, tf=512, first_step_chunks=4):
    n, d = x.shape
    f = w_out.shape[0]
    nt, nk = n // tm, f // tf
    in_specs = [
        pl.BlockSpec((tm, d), lambda i, k: (i, 0)),
        pl.BlockSpec((None, 1, d), lambda i, k: (layer, 0, 0)),
        pl.BlockSpec((d, tf), lambda i, k: (0, k)),
        pl.BlockSpec((d, tf), lambda i, k: (0, k + nk)),
        pl.BlockSpec((tf, d), lambda i, k: (k, 0)),
    ]
    args = [x, gain.reshape(gain.shape[0], 1, d), w_in, w_in, w_out]
    out_specs = [pl.BlockSpec((tm, d), lambda i, k: (i, 0))]
    out_shape = [jax.ShapeDtypeStruct((n, d), F32)]
    if final_gain is not None:
        in_specs.append(pl.BlockSpec((1, d), lambda i, k: (0, 0)))
        args.append(final_gain.reshape(1, d))
    if next_w_in is not None:
        in_tile = (d // nt, 2 * f // nk)
        in_specs.append(pl.BlockSpec((None,) + in_tile, lambda i, k: (layer + 1, i, k)))
        args.append(next_w_in)
        out_specs.append(pl.BlockSpec(in_tile, lambda i, k: (i, k)))
        out_shape.append(jax.ShapeDtypeStruct((d, 2 * f), BF16))
    return pl.pallas_call(
        functools.partial(_ffn_kernel, first_step_chunks=first_step_chunks,
                          has_final=final_gain is not None, n_cast=0 if next_w_in is None else 1),
        grid=(nt, nk),
        in_specs=in_specs,
        out_specs=out_specs,
        out_shape=out_shape,
        scratch_shapes=[pltpu.VMEM((tm, d), BF16)],
        compiler_params=_params(2),
        name="ffn",
    )(*args)


def _gelu_tanh(x):
    return 0.5 * x * (1.0 + jnp.tanh(0.7978845608028654 * (x + 0.044715 * (x * x * x))))


def _mix_in_kernel(x_ref, g_ref, wm_ref, wl_ref, wf_ref, fb_ref, cw_ref, lw_ref, lb_ref, wax_ref, ba_ref,
                   bx_ref, lam_ref, yc_ref, q_ref, k_ref, v_ref, yl_ref, cum_ref,
                   carry_ref, scan_ref, cb_ref, lg_ref, zbuf, xbuf, abuf, ubuf, h_ref,
                   *, chunk, q_scale, tiles_per_seq):
    tt, dc, dl = x_ref.shape[0], yc_ref.shape[1], yl_ref.shape[1]
    o = SUBLANES
    pad = abuf.shape[0] - tt

    @pl.when(pl.program_id(0) % tiles_per_seq == 0)
    def _():
        carry_ref[...] = jnp.zeros_like(carry_ref)
        zbuf[0:o, :] = jnp.zeros((o, dc), F32)
        xbuf[0:o, :] = jnp.zeros((o, dl), F32)
        h_ref[...] = jnp.zeros_like(h_ref)
        abuf[0:pad, :] = jnp.ones((pad, dl), F32)
        ubuf[0:pad, :] = jnp.zeros((pad, dl), F32)

    xn = _rms(x_ref[...], g_ref[...]).astype(BF16)

    def proj(w_ref, col, width):
        return jnp.dot(xn, w_ref[:, col:col + width], preferred_element_type=F32)

    def conv3():
        yc = cw_ref[CONV_WIDTH - 1:CONV_WIDTH, :] * zbuf[o:o + tt, :]
        for s in range(1, CONV_WIDTH):
            yc = yc + cw_ref[CONV_WIDTH - 1 - s:CONV_WIDTH - s, :] * zbuf[o - s:o - s + tt, :]
        yc_ref[...] = cb_ref[...] * yc
        zbuf[0:o, :] = zbuf[tt:tt + o, :]

    def conv4():
        xr = lw_ref[LRU_CONV_WIDTH - 1:LRU_CONV_WIDTH, :] * xbuf[o:o + tt, :]
        for s in range(1, LRU_CONV_WIDTH):
            xr = xr + lw_ref[LRU_CONV_WIDTH - 1 - s:LRU_CONV_WIDTH - s, :] * xbuf[o - s:o - s + tt, :]
        xr = xr + lb_ref[...]
        xbuf[0:o, :] = xbuf[tt:tt + o, :]
        xbuf[o:o + tt, :] = xr

    blk = dl // LRU_BLOCKS
    lam = lam_ref[...]
    softplus_neg_lam = jnp.maximum(-lam, 0.0) + jnp.log1p(jnp.exp(-jnp.abs(lam)))

    def lru_inputs(g):
        sl = slice(g * blk, (g + 1) * blk)
        xg = xbuf[o:o + tt, sl]
        xg_bf = xg.astype(BF16)
        gates = jnp.dot(xg_bf, wax_ref[g], preferred_element_type=F32)
        r = jax.nn.sigmoid(gates[:, :blk] + ba_ref[:, sl])
        ig = jax.nn.sigmoid(gates[:, blk:] + bx_ref[:, sl])
        log_a = (-LRU_C * r) * softplus_neg_lam[:, sl]
        th = jnp.tanh(log_a)
        mult = jnp.sqrt(-2.0 * th / (1.0 - th))
        abuf[pad:pad + tt, sl] = jnp.exp(log_a)
        ubuf[pad:pad + tt, sl] = mult * (ig * xg)

    def scan_step(s):
        a = abuf[pad:pad + tt, :]
        u = ubuf[pad:pad + tt, :]
        a_sh = abuf[pad - s:pad - s + tt, :]
        u_sh = ubuf[pad - s:pad - s + tt, :]
        ubuf[pad:pad + tt, :] = a * u_sh + u
        abuf[pad:pad + tt, :] = a * a_sh

    def finish_lru():
        hs = abuf[pad:pad + tt, :] * h_ref[0:1, :] + ubuf[pad:pad + tt, :]
        h_ref[...] = jnp.broadcast_to(hs[tt - 1:tt, :], h_ref.shape)
        yl_ref[...] = _gelu_tanh(lg_ref[...]) * hs

    def store(out, c, w, scale=None):
        def run(col):
            y = proj(wm_ref, col, w)
            out[:, c:c + w] = (y if scale is None else y * scale).astype(out.dtype)
        return run

    def store_cb(col):
        cb_ref[...] = proj(wm_ref, col, dc)

    def store_lg(col):
        lg_ref[...] = proj(wl_ref, 0, dl)

    zbuf[o:o + tt, :] = proj(wm_ref, dc, dc) * proj(wm_ref, 2 * dc, dc)
    xbuf[o:o + tt, :] = proj(wl_ref, dl, dl)
    top = scan_ref.shape[0] - tt
    rows = slice(top, top + tt)
    z = jnp.dot(xn, wf_ref[...], preferred_element_type=F32) + fb_ref[...]
    scan_ref[rows, :] = -(jnp.maximum(-z, 0.0) + jnp.log1p(jnp.exp(-jnp.abs(z))))

    def forget_decay():
        scan_ref[0:top, :] = jnp.zeros((top, scan_ref.shape[1]), F32)
        shift = 1
        while shift < tt:
            scan_ref[rows, :] = scan_ref[rows, :] + scan_ref[top - shift:top - shift + tt, :]
            shift *= 2
        cum = scan_ref[rows, :] + carry_ref[0:1, :]
        cum_ref[...] = cum
        carry_ref[...] = jnp.broadcast_to(cum[tt - 1:tt, :], carry_ref.shape)

    mxu_work = [(store_cb, 0), (store_lg, 0)]
    col = 3 * dc
    for out, scale in ((q_ref, q_scale), (k_ref, None), (v_ref, None)):
        for c in range(0, out.shape[1], chunk):
            w = min(chunk, out.shape[1] - c)
            mxu_work.append((store(out, c, w, scale), col + c))
        col += out.shape[1]
    scans = []
    s = 1
    while s < tt:
        scans.append(functools.partial(scan_step, s))
        s *= 2
    vpu_work = ([[conv4], [functools.partial(lru_inputs, 0)], [conv3, forget_decay]]
                + [[functools.partial(lru_inputs, g)] for g in range(1, LRU_BLOCKS)]
                + [scans[:len(scans) // 2], scans[len(scans) // 2:] + [finish_lru]])
    assert len(mxu_work) >= len(vpu_work)
    for n, (dot, wcol) in enumerate(mxu_work):
        dot(wcol)
        for piece in (vpu_work[n] if n < len(vpu_work) else []):
            piece()


def _row_slab_cast(w_stack, layer, steps):
    slab = (w_stack.shape[1] // steps, w_stack.shape[2])
    assert slab[0] * steps == w_stack.shape[1] and slab[0] % (2 * SUBLANES) == 0
    return (pl.BlockSpec((None,) + slab, lambda i: (layer, i, 0)), pl.BlockSpec(slab, lambda i: (i, 0)),
            jax.ShapeDtypeStruct(w_stack.shape[1:], BF16))


def _mix_in(x, gain, w_all, w_lru, w_f, f_bias, conv_w, lru_conv_w, lru_conv_b, w_ax, b_a, b_x, lam,
            layer, d_att, q_scale, seq, tm=256, chunk=512):
    n, d = x.shape
    d_conv, d_lru, f_lanes = conv_w.shape[2], lam.shape[1], w_f.shape[2]
    main = 3 * d_conv + 3 * d_att
    widths = (d_conv, d_att, d_att, d_att, d_lru, f_lanes)
    dtypes = (F32, BF16, BF16, BF16, F32, F32)
    pad = tm // 2
    per_layer = lambda a: pl.BlockSpec((None,) + a.shape[1:], lambda i: (layer,) + (0,) * (a.ndim - 1))
    resident = lambda shape: pl.BlockSpec((None,) + shape, lambda i: (layer, 0, 0),
                                          pipeline_mode=pl.Buffered(1))
    rows3 = lambda a: a.reshape(a.shape[0], 1, a.shape[1])
    small = [conv_w, lru_conv_w, rows3(lru_conv_b), w_ax, rows3(b_a), rows3(b_x), rows3(lam)]
    return pl.pallas_call(
        functools.partial(_mix_in_kernel, chunk=chunk, q_scale=q_scale, tiles_per_seq=seq // tm),
        grid=(n // tm,),
        in_specs=[
            pl.BlockSpec((tm, d), lambda i: (i, 0)),
            pl.BlockSpec((None, 1, d), lambda i: (layer, 0, 0)),
            resident((d, main)), resident(w_lru.shape[1:]), resident(w_f.shape[1:]),
            pl.BlockSpec((None, 1, f_lanes), lambda i: (layer, 0, 0)),
        ] + [per_layer(a) for a in small],
        out_specs=[pl.BlockSpec((tm, wd), lambda i: (i, 0)) for wd in widths],
        out_shape=[jax.ShapeDtypeStruct((n, wd), dt) for wd, dt in zip(widths, dtypes)],
        scratch_shapes=[
            pltpu.VMEM((SUBLANES, f_lanes), F32),
            pltpu.VMEM((pad + tm, f_lanes), F32),
            pltpu.VMEM((tm, d_conv), F32),
            pltpu.VMEM((tm, d_lru), F32),
            pltpu.VMEM((tm + SUBLANES, d_conv), F32),
            pltpu.VMEM((tm + SUBLANES, d_lru), F32),
            pltpu.VMEM((pad + tm, d_lru), F32),
            pltpu.VMEM((pad + tm, d_lru), F32),
            pltpu.VMEM((SUBLANES, d_lru), F32),
        ],
        compiler_params=_params(1),
        name="mix_in",
    )(x, gain.reshape(gain.shape[0], 1, d), w_all, w_lru, w_f, rows3(f_bias), *small)


DECAY_COPIES = 6


def _decay_split(c_block):
    c = c_block * LOG2E
    hi = c.astype(BF16).astype(F32)
    rest = c - hi
    mid = rest.astype(BF16).astype(F32)
    return hi, mid, rest - mid


def _decay_columns(pieces, head, nh, on_query_side):
    hi, mid, lo = pieces
    lane = lax.broadcasted_iota(jnp.int32, hi.shape, 1)
    onehot = jnp.where(lane % nh == head, 1.0, 0.0)
    first = 0 if on_query_side else 3 * nh
    split = jnp.where(lane < first + nh, hi, jnp.where(lane < first + 2 * nh, mid, lo))
    if on_query_side:
        out = jnp.where(lane < 3 * nh, split, jnp.where(lane < 6 * nh, onehot, 0.0))
    else:
        out = jnp.where(lane < 3 * nh, onehot, jnp.where(lane < 6 * nh, -split, 0.0))
    return out.astype(BF16)


def _attn_kernel(q_ref, k_ref, v_ref, cq_ref, ck_ref, o_ref, kx_ref, s_ref, *, tq, nh, build_rows):
    i = pl.program_id(2)
    dh = HEAD_DIM
    hp = q_ref.shape[1] // dh
    heads = [(pl.program_id(1) * hp + g, slice(g * dh, (g + 1) * dh)) for g in range(hp)]

    @pl.when(i == 0)
    def _():
        def build(c, carry):
            rows = pl.ds(pl.multiple_of(c * build_rows, build_rows), build_rows)
            pieces = _decay_split(ck_ref[rows, :])
            for g, (head, cols) in enumerate(heads):
                kx_ref[g, rows, 0:dh] = k_ref[rows, cols]
                kx_ref[g, rows, dh:2 * dh] = _decay_columns(pieces, head, nh, on_query_side=False)
            return carry
        lax.fori_loop(0, k_ref.shape[0] // build_rows, build, 0)

    def key_rows(j):
        return pl.ds(pl.multiple_of(j * tq, tq), tq)

    pieces = _decay_split(cq_ref[...])
    qx_t = []
    for g, (head, cols) in enumerate(heads):
        qx = jnp.concatenate([q_ref[:, cols].astype(F32),
                              _decay_columns(pieces, head, nh, on_query_side=True).astype(F32)], axis=1)
        qx_t.append(qx.T.astype(BF16))
        s_ref[0, g] = jnp.dot(kx_ref[g, key_rows(0), :], qx_t[g], preferred_element_type=F32)

    def score(j, slot):
        for g in range(hp):
            s_ref[slot, g] = jnp.dot(kx_ref[g, key_rows(j), :], qx_t[g], preferred_element_type=F32)

    def absorb(j, slot, carries, masked=False):
        probs, new = [], []
        for g in range(hp):
            m, l, acc = carries[g]
            s = s_ref[slot, g]
            if masked:
                r = lax.broadcasted_iota(jnp.int32, s.shape, 0)
                c = lax.broadcasted_iota(jnp.int32, s.shape, 1)
                s = jnp.where(r <= c, s, -jnp.inf)
            m_new = jnp.maximum(m, jnp.max(s, axis=0, keepdims=True))
            alpha = jnp.exp2(m - m_new)
            p = jnp.exp2(s - m_new)
            probs.append(p.astype(BF16))
            new.append((m_new, alpha * l + jnp.sum(p, axis=0, keepdims=True), alpha * acc))
        out = []
        for g, p in enumerate(probs):
            pv = lax.dot_general(v_ref[key_rows(j), heads[g][1]], p, (((0,), (0,)), ((), ())),
                                 preferred_element_type=F32)
            out.append((new[g][0], new[g][1], new[g][2] + pv))
        return tuple(out)

    def two_blocks(pair, carries):
        j = 2 * pair
        score(j + 1, 1)
        carries = absorb(j, 0, carries)
        score(j + 2, 0)
        return absorb(j + 1, 1, carries)

    def last_block(carries):
        return absorb(i, 0, carries, masked=True)

    def last_two_blocks(carries):
        score(i, 1)
        return absorb(i, 1, absorb(i - 1, 0, carries), masked=True)

    init = (jnp.full((1, tq), -jnp.inf, F32), jnp.zeros((1, tq), F32), jnp.zeros((dh, tq), F32))
    carries = lax.fori_loop(0, i // 2, two_blocks, (init,) * hp)
    carries = lax.cond(i % 2 == 0, last_block, last_two_blocks, carries)
    for g, (_, cols) in enumerate(heads):
        m, l, acc = carries[g]
        o_ref[:, cols] = (acc * (1.0 / l)).T


def _attention(q, k, v, cum, batch, seq, tq=512, hp=4, build_rows=512):
    n, d_att = q.shape
    nh = d_att // HEAD_DIM
    nq = seq // tq
    width = hp * HEAD_DIM
    return pl.pallas_call(
        functools.partial(_attn_kernel, tq=tq, nh=nh, build_rows=build_rows),
        grid=(batch, nh // hp, nq),
        in_specs=[
            pl.BlockSpec((tq, width), lambda b, h, i: (b * nq + i, h)),
            pl.BlockSpec((seq, width), lambda b, h, i: (b, h)),
            pl.BlockSpec((seq, width), lambda b, h, i: (b, h)),
            pl.BlockSpec((tq, LANES), lambda b, h, i: (b * nq + i, 0)),
            pl.BlockSpec((seq, LANES), lambda b, h, i: (b, 0)),
        ],
        out_specs=pl.BlockSpec((tq, width), lambda b, h, i: (b * nq + i, h)),
        out_shape=jax.ShapeDtypeStruct((n, d_att), F32),
        scratch_shapes=[pltpu.VMEM((hp, seq, 2 * HEAD_DIM), BF16), pltpu.VMEM((2, hp, tq, tq), F32)],
        compiler_params=_params(3),
        name="fox_attention",
    )(q, k, v, cum, cum)


def _mix_out_kernel(x_ref, yc_ref, ya_ref, yl_ref, g_ref, w_ref, *rest, n_cast):
    cast_src, o_ref, cast_dst, y_ref = rest[:n_cast], rest[n_cast], rest[n_cast + 1:2 * n_cast + 1], rest[-1]
    _cast_pieces(cast_src, cast_dst)
    col = 0
    for src in (yc_ref, ya_ref, yl_ref):
        width = src.shape[1]
        y_ref[:, col:col + width] = _rms(src[...], g_ref[:, col:col + width]).astype(BF16)
        col += width
    o_ref[...] = x_ref[...] + jnp.dot(y_ref[...], w_ref[...], preferred_element_type=F32)


def _mix_out(x, yc, ya, yl, gain, w, layer, casts=(), tm=512):
    n, d = x.shape
    dm = w.shape[0]
    row = lambda i: (i, 0)
    in_specs = [
        pl.BlockSpec((tm, d), row),
        pl.BlockSpec((tm, yc.shape[1]), row),
        pl.BlockSpec((tm, ya.shape[1]), row),
        pl.BlockSpec((tm, yl.shape[1]), row),
        pl.BlockSpec((None, 1, dm), lambda i: (layer, 0, 0)),
        pl.BlockSpec(w.shape, lambda i: (0, 0), pipeline_mode=pl.Buffered(1)),
    ]
    args = [x, yc, ya, yl, gain.reshape(gain.shape[0], 1, dm), w]
    out_specs = [pl.BlockSpec((tm, d), row)]
    out_shape = [jax.ShapeDtypeStruct((n, d), F32)]
    for w_stack, which in casts:
        src_spec, dst_spec, dst_shape = _row_slab_cast(w_stack, which, n // tm)
        in_specs.append(src_spec)
        args.append(w_stack)
        out_specs.append(dst_spec)
        out_shape.append(dst_shape)
    return pl.pallas_call(
        functools.partial(_mix_out_kernel, n_cast=len(casts)),
        grid=(n // tm,),
        in_specs=in_specs,
        out_specs=out_specs,
        out_shape=out_shape,
        scratch_shapes=[pltpu.VMEM((tm, dm), BF16)],
        compiler_params=_params(1),
        name="mix_out",
    )(*args)


def kernel(x, norm_ffn1, ffn1_w_in, ffn1_w_out, norm_mix, mix_w_in, conv_w, fgate_b, lru_conv_w,
           lru_conv_b, lru_w_a, lru_b_a, lru_w_x, lru_b_x, lru_lambda, mix_out_norm, mix_w_out,
           norm_ffn2, ffn2_w_in, ffn2_w_out, final_norm):
    batch, seq, d = x.shape
    depth = norm_ffn1.shape[0]
    d_conv = conv_w.shape[2]
    d_lru = lru_lambda.shape[1]
    d_att = mix_w_out.shape[1] - d_conv - d_lru
    nh = d_att // HEAD_DIM
    assert nh == fgate_b.shape[1] and DECAY_COPIES * nh <= LANES

    f0 = 3 * d_conv + 3 * d_att
    f_pad = LANES - DECAY_COPIES * nh
    fbias = jnp.pad(jnp.tile(fgate_b, (1, DECAY_COPIES)), ((0, 0), (0, f_pad)))
    w_ax = jnp.concatenate([lru_w_a, lru_w_x], axis=3).astype(BF16)
    q_scale = LOG2E * HEAD_DIM ** -0.5

    w1_in, w1_out = ffn1_w_in[0].astype(BF16), ffn1_w_out[0].astype(BF16)
    w2_in = ffn2_w_in[0].astype(BF16)
    w_mix_out = mix_w_out[0].astype(BF16)
    w_mix = mix_w_in.astype(BF16)
    w_mix_lru = w_mix[:, :, f0 + nh:]
    w_mix_f = jnp.pad(jnp.tile(w_mix[:, :, f0:f0 + nh], (1, 1, DECAY_COPIES)), ((0, 0), (0, 0), (0, f_pad)))

    h = x.reshape(batch * seq, d)
    for l in range(depth):
        more = l + 1 < depth
        h, *w1_in_next = _ffn(h, norm_ffn1, w1_in, w1_out, l, next_w_in=ffn1_w_in if more else None)
        yc, q, k, v, yl, cum = _mix_in(h, norm_mix, w_mix, w_mix_lru, w_mix_f, fbias, conv_w, lru_conv_w,
                                       lru_conv_b, w_ax, lru_b_a, lru_b_x, lru_lambda, l, d_att,
                                       q_scale, seq)
        ya = _attention(q, k, v, cum, batch, seq)
        casts = [(ffn2_w_out, l)] + ([(ffn1_w_out, l + 1), (mix_w_out, l + 1)] if more else [])
        h, w2_out, *next_outs = _mix_out(h, yc, ya, yl, mix_out_norm, w_mix_out, l, casts=casts)
        h, *w2_in_next = _ffn(h, norm_ffn2, w2_in, w2_out, l, next_w_in=ffn2_w_in if more else None,
                              final_gain=None if more else final_norm)
        if more:
            w1_in, w2_in = w1_in_next[0], w2_in_next[0]
            w1_out, w_mix_out = next_outs
    return h.reshape(batch, seq, d)
```

```python
import functools

import jax
import jax.numpy as jnp
from jax import lax
from jax.experimental import pallas as pl
from jax.experimental.pallas import tpu as pltpu

F32 = jnp.float32
BF16 = jnp.bfloat16

EPS = 1e-6
LOG2E = 1.4426950408889634
HEAD_DIM = 128
LRU_BLOCKS = 4
LRU_C = 8.0
CONV_WIDTH = 3
LRU_CONV_WIDTH = 4
LANES = 128
SUBLANES = 8
VMEM_LIMIT = 58 * 1024 * 1024


def _params(n_axes, vmem=VMEM_LIMIT, **extra):
    return pltpu.CompilerParams(dimension_semantics=("arbitrary",) * n_axes,
                                vmem_limit_bytes=vmem, **extra)


def _rms(x, gain):
    ms = jnp.mean(x * x, axis=-1, keepdims=True)
    return (x * lax.rsqrt(ms + EPS)) * gain


def _cast_pieces(src_refs, dst_refs):
    for src, dst in zip(src_refs, dst_refs):
        dst[...] = src[...].astype(dst.dtype)


def _ffn_kernel(x_ref, g_ref, wg_ref, wu_ref, wo_ref, *rest, first_step_chunks, has_final, n_cast):
    rest = list(rest)
    gf_ref = rest.pop(0) if has_final else None
    cast_src, o_ref, cast_dst, xn_ref = rest[:n_cast], rest[n_cast], rest[n_cast + 1:2 * n_cast + 1], rest[-1]
    _cast_pieces(cast_src, cast_dst)
    k = pl.program_id(1)

    def swiglu_half(xn):
        g = jnp.dot(xn, wg_ref[...], preferred_element_type=F32)
        u = jnp.dot(xn, wu_ref[...], preferred_element_type=F32)
        h = (((g * jax.nn.sigmoid(g)) * u) * 0.5).astype(BF16)
        return jnp.dot(h, wo_ref[...], preferred_element_type=F32)

    @pl.when(k == 0)
    def _():
        rc = x_ref.shape[0] // first_step_chunks
        for c in range(first_step_chunks):
            rows = slice(c * rc, (c + 1) * rc)
            x = x_ref[rows, :]
            xn = _rms(x, g_ref[...]).astype(BF16)
            xn_ref[rows, :] = xn
            o_ref[rows, :] = x + swiglu_half(xn)

    @pl.when(k != 0)
    def _():
        o_ref[...] += swiglu_half(xn_ref[...])

    if gf_ref is not None:
        @pl.when(k == pl.num_programs(1) - 1)
        def _():
            o_ref[...] = _rms(o_ref[...], gf_ref[...])


def _ffn(x, gain, w_in, w_out, layer, next_w_in=None, final_gain=None, tm=1024, tf=512, first_step_chunks=4):
    n, d = x.shape
    f = w_out.shape[0]
    nt, nk = n // tm, f // tf
    in_specs = [
        pl.BlockSpec((tm, d), lambda i, k: (i, 0)),
        pl.BlockSpec((None, 1, d), lambda i, k: (layer, 0, 0)),
        pl.BlockSpec((d, tf), lambda i, k: (0, k)),
        pl.BlockSpec((d, tf), lambda i, k: (0, k + nk)),
        pl.BlockSpec((tf, d), lambda i, k: (k, 0)),
    ]
    args = [x, gain.reshape(gain.shape[0], 1, d), w_in, w_in, w_out]
    out_specs = [pl.BlockSpec((tm, d), lambda i, k: (i, 0))]
    out_shape = [jax.ShapeDtypeStruct((n, d), F32)]
    if final_gain is not None:
        in_specs.append(pl.BlockSpec((1, d), lambda i, k: (0, 0)))
        args.append(final_gain.reshape(1, d))
    if next_w_in is not None:
        in_tile = (d // nt, 2 * f // nk)
        in_specs.append(pl.BlockSpec((None,) + in_tile, lambda i, k: (layer + 1, i, k)))
        args.append(next_w_in)
        out_specs.append(pl.BlockSpec(in_tile, lambda i, k: (i, k)))
        out_shape.append(jax.ShapeDtypeStruct((d, 2 * f), BF16))
    return pl.pallas_call(
        functools.partial(_ffn_kernel, first_step_chunks=first_step_chunks,
                          has_final=final_gain is not None, n_cast=0 if next_w_in is None else 1),
        grid=(nt, nk),
        in_specs=in_specs,
        out_specs=out_specs,
        out_shape=out_shape,
        scratch_shapes=[pltpu.VMEM((tm, d), BF16)],
        compiler_params=_params(2),
        name="ffn",
    )(*args)


def _gelu_tanh(x):
    return 0.5 * x * (1.0 + jnp.tanh(0.7978845608028654 * (x + 0.044715 * (x * x * x))))


def _mix_in_kernel(x_ref, g_ref, wm_ref, wl_ref, wf_ref, fb_ref, cw_ref, lw_ref, lb_ref, wax_ref, ba_ref,
                   bx_ref, lam_ref, yc_ref, q_ref, k_ref, v_ref, yl_ref, cum_ref,
                   carry_ref, scan_ref, cb_ref, lg_ref, zbuf, xbuf, abuf, ubuf, h_ref,
                   *, chunk, q_scale, tiles_per_seq):
    tt, dc, dl = x_ref.shape[0], yc_ref.shape[1], yl_ref.shape[1]
    o = SUBLANES
    pad = abuf.shape[0] - tt

    @pl.when(pl.program_id(0) % tiles_per_seq == 0)
    def _():
        carry_ref[...] = jnp.zeros_like(carry_ref)
        zbuf[0:o, :] = jnp.zeros((o, dc), F32)
        xbuf[0:o, :] = jnp.zeros((o, dl), F32)
        h_ref[...] = jnp.zeros_like(h_ref)
        abuf[0:pad, :] = jnp.ones((pad, dl), F32)
        ubuf[0:pad, :] = jnp.zeros((pad, dl), F32)

    xn = _rms(x_ref[...], g_ref[...]).astype(BF16)

    def proj(w_ref, col, width):
        return jnp.dot(xn, w_ref[:, col:col + width], preferred_element_type=F32)

    def conv3():
        yc = cw_ref[CONV_WIDTH - 1:CONV_WIDTH, :] * zbuf[o:o + tt, :]
        for s in range(1, CONV_WIDTH):
            yc = yc + cw_ref[CONV_WIDTH - 1 - s:CONV_WIDTH - s, :] * zbuf[o - s:o - s + tt, :]
        yc_ref[...] = cb_ref[...] * yc
        zbuf[0:o, :] = zbuf[tt:tt + o, :]

    def conv4():
        xr = lw_ref[LRU_CONV_WIDTH - 1:LRU_CONV_WIDTH, :] * xbuf[o:o + tt, :]
        for s in range(1, LRU_CONV_WIDTH):
            xr = xr + lw_ref[LRU_CONV_WIDTH - 1 - s:LRU_CONV_WIDTH - s, :] * xbuf[o - s:o - s + tt, :]
        xr = xr + lb_ref[...]
        xbuf[0:o, :] = xbuf[tt:tt + o, :]
        xbuf[o:o + tt, :] = xr

    blk = dl // LRU_BLOCKS
    lam = lam_ref[...]
    softplus_neg_lam = jnp.maximum(-lam, 0.0) + jnp.log1p(jnp.exp(-jnp.abs(lam)))

    def lru_inputs(g):
        sl = slice(g * blk, (g + 1) * blk)
        xg = xbuf[o:o + tt, sl]
        xg_bf = xg.astype(BF16)
        gates = jnp.dot(xg_bf, wax_ref[g], preferred_element_type=F32)
        r = jax.nn.sigmoid(gates[:, :blk] + ba_ref[:, sl])
        ig = jax.nn.sigmoid(gates[:, blk:] + bx_ref[:, sl])
        log_a = (-LRU_C * r) * softplus_neg_lam[:, sl]
        th = jnp.tanh(log_a)
        mult = jnp.sqrt(-2.0 * th / (1.0 - th))
        abuf[pad:pad + tt, sl] = jnp.exp(log_a)
        ubuf[pad:pad + tt, sl] = mult * (ig * xg)

    def scan_step(s):
        a = abuf[pad:pad + tt, :]
        u = ubuf[pad:pad + tt, :]
        a_sh = abuf[pad - s:pad - s + tt, :]
        u_sh = ubuf[pad - s:pad - s + tt, :]
        ubuf[pad:pad + tt, :] = a * u_sh + u
        abuf[pad:pad + tt, :] = a * a_sh

    def finish_lru():
        hs = abuf[pad:pad + tt, :] * h_ref[0:1, :] + ubuf[pad:pad + tt, :]
        h_ref[...] = jnp.broadcast_to(hs[tt - 1:tt, :], h_ref.shape)
        yl_ref[...] = _gelu_tanh(lg_ref[...]) * hs

    def store(out, c, w, scale=None):
        def run(col):
            y = proj(wm_ref, col, w)
            out[:, c:c + w] = (y if scale is None else y * scale).astype(out.dtype)
        return run

    def store_cb(col):
        cb_ref[...] = proj(wm_ref, col, dc)

    def store_lg(col):
        lg_ref[...] = proj(wl_ref, 0, dl)

    zbuf[o:o + tt, :] = proj(wm_ref, dc, dc) * proj(wm_ref, 2 * dc, dc)
    xbuf[o:o + tt, :] = proj(wl_ref, dl, dl)
    top = scan_ref.shape[0] - tt
    rows = slice(top, top + tt)
    z = jnp.dot(xn, wf_ref[...], preferred_element_type=F32) + fb_ref[...]
    scan_ref[rows, :] = -(jnp.maximum(-z, 0.0) + jnp.log1p(jnp.exp(-jnp.abs(z))))

    def forget_decay():
        scan_ref[0:top, :] = jnp.zeros((top, scan_ref.shape[1]), F32)
        shift = 1
        while shift < tt:
            scan_ref[rows, :] = scan_ref[rows, :] + scan_ref[top - shift:top - shift + tt, :]
            shift *= 2
        cum = scan_ref[rows, :] + carry_ref[0:1, :]
        cum_ref[...] = cum
        carry_ref[...] = jnp.broadcast_to(cum[tt - 1:tt, :], carry_ref.shape)

    mxu_work = [(store_cb, 0), (store_lg, 0)]
    col = 3 * dc
    for out, scale in ((q_ref, q_scale), (k_ref, None), (v_ref, None)):
        for c in range(0, out.shape[1], chunk):
            w = min(chunk, out.shape[1] - c)
            mxu_work.append((store(out, c, w, scale), col + c))
        col += out.shape[1]
    scans = []
    s = 1
    while s < tt:
        scans.append(functools.partial(scan_step, s))
        s *= 2
    vpu_work = ([[conv4], [functools.partial(lru_inputs, 0)], [conv3, forget_decay]]
                + [[functools.partial(lru_inputs, g)] for g in range(1, LRU_BLOCKS)]
                + [scans[:len(scans) // 2], scans[len(scans) // 2:] + [finish_lru]])
    assert len(mxu_work) >= len(vpu_work)
    for n, (dot, wcol) in enumerate(mxu_work):
        dot(wcol)
        for piece in (vpu_work[n] if n < len(vpu_work) else []):
            piece()


def _row_slab_cast(w_stack, layer, steps):
    slab = (w_stack.shape[1] // steps, w_stack.shape[2])
    assert slab[0] * steps == w_stack.shape[1] and slab[0] % (2 * SUBLANES) == 0
    return (pl.BlockSpec((None,) + slab, lambda i: (layer, i, 0)), pl.BlockSpec(slab, lambda i: (i, 0)),
            jax.ShapeDtypeStruct(w_stack.shape[1:], BF16))


def _mix_in(x, gain, w_all, w_lru, w_f, f_bias, conv_w, lru_conv_w, lru_conv_b, w_ax, b_a, b_x, lam,
            layer, d_att, q_scale, seq, tm=256, chunk=512):
    n, d = x.shape
    d_conv, d_lru, f_lanes = conv_w.shape[2], lam.shape[1], w_f.shape[2]
    main = 3 * d_conv + 3 * d_att
    widths = (d_conv, d_att, d_att, d_att, d_lru, f_lanes)
    dtypes = (F32, BF16, BF16, BF16, F32, F32)
    pad = tm // 2
    per_layer = lambda a: pl.BlockSpec((None,) + a.shape[1:], lambda i: (layer,) + (0,) * (a.ndim - 1))
    resident = lambda shape: pl.BlockSpec((None,) + shape, lambda i: (layer, 0, 0),
                                          pipeline_mode=pl.Buffered(1))
    rows3 = lambda a: a.reshape(a.shape[0], 1, a.shape[1])
    small = [conv_w, lru_conv_w, rows3(lru_conv_b), w_ax, rows3(b_a), rows3(b_x), rows3(lam)]
    return pl.pallas_call(
        functools.partial(_mix_in_kernel, chunk=chunk, q_scale=q_scale, tiles_per_seq=seq // tm),
        grid=(n // tm,),
        in_specs=[
            pl.BlockSpec((tm, d), lambda i: (i, 0)),
            pl.BlockSpec((None, 1, d), lambda i: (layer, 0, 0)),
            resident((d, main)), resident(w_lru.shape[1:]), resident(w_f.shape[1:]),
            pl.BlockSpec((None, 1, f_lanes), lambda i: (layer, 0, 0)),
        ] + [per_layer(a) for a in small],
        out_specs=[pl.BlockSpec((tm, wd), lambda i: (i, 0)) for wd in widths],
        out_shape=[jax.ShapeDtypeStruct((n, wd), dt) for wd, dt in zip(widths, dtypes)],
        scratch_shapes=[
            pltpu.VMEM((SUBLANES, f_lanes), F32),
            pltpu.VMEM((pad + tm, f_lanes), F32),
            pltpu.VMEM((tm, d_conv), F32),
            pltpu.VMEM((tm, d_lru), F32),
            pltpu.VMEM((tm + SUBLANES, d_conv), F32),
            pltpu.VMEM((tm + SUBLANES, d_lru), F32),
            pltpu.VMEM((pad + tm, d_lru), F32),
            pltpu.VMEM((pad + tm, d_lru), F32),
            pltpu.VMEM((SUBLANES, d_lru), F32),
        ],
        compiler_params=_params(1),
        name="mix_in",
    )(x, gain.reshape(gain.shape[0], 1, d), w_all, w_lru, w_f, rows3(f_bias), *small)


DECAY_COPIES = 6


def _decay_split(c_block):
    c = c_block * LOG2E
    hi = c.astype(BF16).astype(F32)
    rest = c - hi
    mid = rest.astype(BF16).astype(F32)
    return hi, mid, rest - mid


def _decay_columns(pieces, head, nh, on_query_side):
    hi, mid, lo = pieces
    lane = lax.broadcasted_iota(jnp.int32, hi.shape, 1)
    onehot = jnp.where(lane % nh == head, 1.0, 0.0)
    first = 0 if on_query_side else 3 * nh
    split = jnp.where(lane < first + nh, hi, jnp.where(lane < first + 2 * nh, mid, lo))
    if on_query_side:
        out = jnp.where(lane < 3 * nh, split, jnp.where(lane < 6 * nh, onehot, 0.0))
    else:
        out = jnp.where(lane < 3 * nh, onehot, jnp.where(lane < 6 * nh, -split, 0.0))
    return out.astype(BF16)


def _attn_kernel(q_ref, k_ref, v_ref, cq_ref, ck_ref, *rest, tq, nh, build_rows, n_cast):
    cast_src, o_ref, cast_dst = rest[:n_cast], rest[n_cast], rest[n_cast + 1:2 * n_cast + 1]
    kx_ref, s_ref = rest[-2:]
    _cast_pieces(cast_src, cast_dst)
    i = pl.program_id(2)
    dh = HEAD_DIM
    hp = q_ref.shape[1] // dh
    heads = [(pl.program_id(1) * hp + g, slice(g * dh, (g + 1) * dh)) for g in range(hp)]

    @pl.when(i == 0)
    def _():
        def build(c, carry):
            rows = pl.ds(pl.multiple_of(c * build_rows, build_rows), build_rows)
            pieces = _decay_split(ck_ref[rows, :])
            for g, (head, cols) in enumerate(heads):
                kx_ref[g, rows, 0:dh] = k_ref[rows, cols]
                kx_ref[g, rows, dh:2 * dh] = _decay_columns(pieces, head, nh, on_query_side=False)
            return carry
        lax.fori_loop(0, k_ref.shape[0] // build_rows, build, 0)

    def key_rows(j):
        return pl.ds(pl.multiple_of(j * tq, tq), tq)

    pieces = _decay_split(cq_ref[...])
    qx_t = []
    for g, (head, cols) in enumerate(heads):
        qx = jnp.concatenate([q_ref[:, cols].astype(F32),
                              _decay_columns(pieces, head, nh, on_query_side=True).astype(F32)], axis=1)
        qx_t.append(qx.T.astype(BF16))
        s_ref[0, g] = jnp.dot(kx_ref[g, key_rows(0), :], qx_t[g], preferred_element_type=F32)

    def score(j, slot):
        for g in range(hp):
            s_ref[slot, g] = jnp.dot(kx_ref[g, key_rows(j), :], qx_t[g], preferred_element_type=F32)

    def absorb(j, slot, carries, masked=False):
        probs, new = [], []
        for g in range(hp):
            m, l, acc = carries[g]
            s = s_ref[slot, g]
            if masked:
                r = lax.broadcasted_iota(jnp.int32, s.shape, 0)
                c = lax.broadcasted_iota(jnp.int32, s.shape, 1)
                s = jnp.where(r <= c, s, -jnp.inf)
            m_new = jnp.maximum(m, jnp.max(s, axis=0, keepdims=True))
            alpha = jnp.exp2(m - m_new)
            p = jnp.exp2(s - m_new)
            probs.append(p.astype(BF16))
            new.append((m_new, alpha * l + jnp.sum(p, axis=0, keepdims=True), alpha * acc))
        out = []
        for g, p in enumerate(probs):
            pv = lax.dot_general(v_ref[key_rows(j), heads[g][1]], p, (((0,), (0,)), ((), ())),
                                 preferred_element_type=F32)
            out.append((new[g][0], new[g][1], new[g][2] + pv))
        return tuple(out)

    def two_blocks(pair, carries):
        j = 2 * pair
        score(j + 1, 1)
        carries = absorb(j, 0, carries)
        score(j + 2, 0)
        return absorb(j + 1, 1, carries)

    def last_block(carries):
        return absorb(i, 0, carries, masked=True)

    def last_two_blocks(carries):
        score(i, 1)
        return absorb(i, 1, absorb(i - 1, 0, carries), masked=True)

    init = (jnp.full((1, tq), -jnp.inf, F32), jnp.zeros((1, tq), F32), jnp.zeros((dh, tq), F32))
    carries = lax.fori_loop(0, i // 2, two_blocks, (init,) * hp)
    carries = lax.cond(i % 2 == 0, last_block, last_two_blocks, carries)
    for g, (_, cols) in enumerate(heads):
        m, l, acc = carries[g]
        o_ref[:, cols] = (acc * (1.0 / l)).T


def _attention(q, k, v, cum, batch, seq, casts=(), tq=512, hp=4, build_rows=512):
    n, d_att = q.shape
    nh = d_att // HEAD_DIM
    nq = seq // tq
    ng = nh // hp
    width = hp * HEAD_DIM
    in_specs = [
        pl.BlockSpec((tq, width), lambda b, h, i: (b * nq + i, h)),
        pl.BlockSpec((seq, width), lambda b, h, i: (b, h)),
        pl.BlockSpec((seq, width), lambda b, h, i: (b, h)),
        pl.BlockSpec((tq, LANES), lambda b, h, i: (b * nq + i, 0)),
        pl.BlockSpec((seq, LANES), lambda b, h, i: (b, 0)),
    ]
    args = [q, k, v, cum, cum]
    out_specs = [pl.BlockSpec((tq, width), lambda b, h, i: (b * nq + i, h))]
    out_shape = [jax.ShapeDtypeStruct((n, d_att), F32)]
    for w_stack, which in casts:
        tile = (w_stack.shape[1] // (batch * ng), w_stack.shape[2] // nq)
        assert tile[0] % (2 * SUBLANES) == 0 and tile[1] % LANES == 0
        in_specs.append(pl.BlockSpec((None,) + tile, lambda b, h, i, which=which: (which, b * ng + h, i)))
        args.append(w_stack)
        out_specs.append(pl.BlockSpec(tile, lambda b, h, i: (b * ng + h, i)))
        out_shape.append(jax.ShapeDtypeStruct(w_stack.shape[1:], BF16))
    return pl.pallas_call(
        functools.partial(_attn_kernel, tq=tq, nh=nh, build_rows=build_rows, n_cast=len(casts)),
        grid=(batch, ng, nq),
        in_specs=in_specs,
        out_specs=out_specs,
        out_shape=out_shape,
        scratch_shapes=[pltpu.VMEM((hp, seq, 2 * HEAD_DIM), BF16), pltpu.VMEM((2, hp, tq, tq), F32)],
        compiler_params=_params(3),
        name="fox_attention",
    )(*args)


def _mix_out_kernel(x_ref, yc_ref, ya_ref, yl_ref, g_ref, w_ref, *rest, n_cast):
    cast_src, o_ref, cast_dst, y_ref = rest[:n_cast], rest[n_cast], rest[n_cast + 1:2 * n_cast + 1], rest[-1]
    _cast_pieces(cast_src, cast_dst)
    col = 0
    for src in (yc_ref, ya_ref, yl_ref):
        width = src.shape[1]
        y_ref[:, col:col + width] = _rms(src[...], g_ref[:, col:col + width]).astype(BF16)
        col += width
    o_ref[...] = x_ref[...] + jnp.dot(y_ref[...], w_ref[...], preferred_element_type=F32)


def _mix_out(x, yc, ya, yl, gain, w, layer, casts=(), tm=512):
    n, d = x.shape
    dm = w.shape[0]
    row = lambda i: (i, 0)
    in_specs = [
        pl.BlockSpec((tm, d), row),
        pl.BlockSpec((tm, yc.shape[1]), row),
        pl.BlockSpec((tm, ya.shape[1]), row),
        pl.BlockSpec((tm, yl.shape[1]), row),
        pl.BlockSpec((None, 1, dm), lambda i: (layer, 0, 0)),
        pl.BlockSpec(w.shape, lambda i: (0, 0), pipeline_mode=pl.Buffered(1)),
    ]
    args = [x, yc, ya, yl, gain.reshape(gain.shape[0], 1, dm), w]
    out_specs = [pl.BlockSpec((tm, d), row)]
    out_shape = [jax.ShapeDtypeStruct((n, d), F32)]
    for w_stack, which in casts:
        src_spec, dst_spec, dst_shape = _row_slab_cast(w_stack, which, n // tm)
        in_specs.append(src_spec)
        args.append(w_stack)
        out_specs.append(dst_spec)
        out_shape.append(dst_shape)
    return pl.pallas_call(
        functools.partial(_mix_out_kernel, n_cast=len(casts)),
        grid=(n // tm,),
        in_specs=in_specs,
        out_specs=out_specs,
        out_shape=out_shape,
        scratch_shapes=[pltpu.VMEM((tm, dm), BF16)],
        compiler_params=_params(1),
        name="mix_out",
    )(*args)


def kernel(x, norm_ffn1, ffn1_w_in, ffn1_w_out, norm_mix, mix_w_in, conv_w, fgate_b, lru_conv_w,
           lru_conv_b, lru_w_a, lru_b_a, lru_w_x, lru_b_x, lru_lambda, mix_out_norm, mix_w_out,
           norm_ffn2, ffn2_w_in, ffn2_w_out, final_norm):
    batch, seq, d = x.shape
    depth = norm_ffn1.shape[0]
    d_conv = conv_w.shape[2]
    d_lru = lru_lambda.shape[1]
    d_att = mix_w_out.shape[1] - d_conv - d_lru
    nh = d_att // HEAD_DIM
    assert nh == fgate_b.shape[1] and DECAY_COPIES * nh <= LANES

    f0 = 3 * d_conv + 3 * d_att
    f_pad = LANES - DECAY_COPIES * nh
    fbias = jnp.pad(jnp.tile(fgate_b, (1, DECAY_COPIES)), ((0, 0), (0, f_pad)))
    w_ax = jnp.concatenate([lru_w_a, lru_w_x], axis=3).astype(BF16)
    q_scale = LOG2E * HEAD_DIM ** -0.5

    w1_in, w1_out = ffn1_w_in[0].astype(BF16), ffn1_w_out[0].astype(BF16)
    w2_in = ffn2_w_in[0].astype(BF16)
    w_mix_out = mix_w_out[0].astype(BF16)
    w_mix = mix_w_in.astype(BF16)
    w_mix_lru = w_mix[:, :, f0 + nh:]
    w_mix_f = jnp.pad(jnp.tile(w_mix[:, :, f0:f0 + nh], (1, 1, DECAY_COPIES)), ((0, 0), (0, 0), (0, f_pad)))

    h = x.reshape(batch * seq, d)
    for l in range(depth):
        more = l + 1 < depth
        h, *w1_in_next = _ffn(h, norm_ffn1, w1_in, w1_out, l, next_w_in=ffn1_w_in if more else None)
        yc, q, k, v, yl, cum = _mix_in(h, norm_mix, w_mix, w_mix_lru, w_mix_f, fbias, conv_w, lru_conv_w,
                                       lru_conv_b, w_ax, lru_b_a, lru_b_x, lru_lambda, l, d_att,
                                       q_scale, seq)
        ya, w2_out, *w1_out_next = _attention(
            q, k, v, cum, batch, seq, casts=[(ffn2_w_out, l)] + ([(ffn1_w_out, l + 1)] if more else []))
        h, *w_mix_out_next = _mix_out(h, yc, ya, yl, mix_out_norm, w_mix_out, l,
                                      casts=[(mix_w_out, l + 1)] if more else [])
        h, *w2_in_next = _ffn(h, norm_ffn2, w2_in, w2_out, l, next_w_in=ffn2_w_in if more else None,
                              final_gain=None if more else final_norm)
        if more:
            w1_in, w2_in = w1_in_next[0], w2_in_next[0]
            w1_out, w_mix_out = w1_out_next[0], w_mix_out_next[0]
    return h.reshape(batch, seq, d)
```

```python
import functools

import jax
import jax.numpy as jnp
from jax import lax
from jax.experimental import pallas as pl
from jax.experimental.pallas import tpu as pltpu

F32 = jnp.float32
BF16 = jnp.bfloat16

EPS = 1e-6
LOG2E = 1.4426950408889634
HEAD_DIM = 128
LRU_BLOCKS = 4
LRU_C = 8.0
CONV_WIDTH = 3
LRU_CONV_WIDTH = 4
LANES = 128
SUBLANES = 8
VMEM_LIMIT = 58 * 1024 * 1024


def _params(n_axes, vmem=VMEM_LIMIT, **extra):
    return pltpu.CompilerParams(dimension_semantics=("arbitrary",) * n_axes,
                                vmem_limit_bytes=vmem, **extra)


def _rms(x, gain):
    ms = jnp.mean(x * x, axis=-1, keepdims=True)
    return (x * lax.rsqrt(ms + EPS)) * gain


def _cast_pieces(src_refs, dst_refs):
    for src, dst in zip(src_refs, dst_refs):
        dst[...] = src[...].astype(dst.dtype)


def _ffn_kernel(x_ref, g_ref, wg_ref, wu_ref, wo_ref, *rest, first_step_chunks):
    (gf_ref, o_ref, xn_ref) = rest if len(rest) == 3 else (None,) + rest
    k = pl.program_id(1)

    def swiglu_half(xn):
        g = jnp.dot(xn, wg_ref[...], preferred_element_type=F32)
        u = jnp.dot(xn, wu_ref[...], preferred_element_type=F32)
        h = (((g * jax.nn.sigmoid(g)) * u) * 0.5).astype(BF16)
        return jnp.dot(h, wo_ref[...], preferred_element_type=F32)

    @pl.when(k == 0)
    def _():
        rc = x_ref.shape[0] // first_step_chunks
        for c in range(first_step_chunks):
            rows = slice(c * rc, (c + 1) * rc)
            x = x_ref[rows, :]
            xn = _rms(x, g_ref[...]).astype(BF16)
            xn_ref[rows, :] = xn
            o_ref[rows, :] = x + swiglu_half(xn)

    @pl.when(k != 0)
    def _():
        o_ref[...] += swiglu_half(xn_ref[...])

    if gf_ref is not None:
        @pl.when(k == pl.num_programs(1) - 1)
        def _():
            o_ref[...] = _rms(o_ref[...], gf_ref[...])


def _ffn(x, gain, w_in, w_out, layer, final_gain=None, tm=1024, tf=512, first_step_chunks=4):
    n, d = x.shape
    f = w_out.shape[0]
    nk = f // tf
    in_specs = [
        pl.BlockSpec((tm, d), lambda i, k: (i, 0)),
        pl.BlockSpec((None, 1, d), lambda i, k: (layer, 0, 0)),
        pl.BlockSpec((d, tf), lambda i, k: (0, k)),
        pl.BlockSpec((d, tf), lambda i, k: (0, k + nk)),
        pl.BlockSpec((tf, d), lambda i, k: (k, 0)),
    ]
    args = [x, gain.reshape(gain.shape[0], 1, d), w_in, w_in, w_out]
    if final_gain is not None:
        in_specs.append(pl.BlockSpec((1, d), lambda i, k: (0, 0)))
        args.append(final_gain.reshape(1, d))
    return pl.pallas_call(
        functools.partial(_ffn_kernel, first_step_chunks=first_step_chunks),
        grid=(n // tm, nk),
        in_specs=in_specs,
        out_specs=pl.BlockSpec((tm, d), lambda i, k: (i, 0)),
        out_shape=jax.ShapeDtypeStruct((n, d), F32),
        scratch_shapes=[pltpu.VMEM((tm, d), BF16)],
        compiler_params=_params(2),
        name="ffn",
    )(*args)


def _gelu_tanh(x):
    return 0.5 * x * (1.0 + jnp.tanh(0.7978845608028654 * (x + 0.044715 * (x * x * x))))


def _mix_in_kernel(x_ref, g_ref, wm_ref, wl_ref, wf_ref, fb_ref, cw_ref, lw_ref, lb_ref, wax_ref, ba_ref,
                   bx_ref, lam_ref, *rest, chunk, q_scale, tiles_per_seq, n_cast):
    cast_src, rest = rest[:n_cast], rest[n_cast:]
    yc_ref, q_ref, k_ref, v_ref, yl_ref, cum_ref = rest[:6]
    cast_dst = rest[6:6 + n_cast]
    carry_ref, scan_ref, cb_ref, lg_ref, zbuf, xbuf, abuf, ubuf, h_ref = rest[6 + n_cast:]
    _cast_pieces(cast_src, cast_dst)
    tt, dc, dl = x_ref.shape[0], yc_ref.shape[1], yl_ref.shape[1]
    o = SUBLANES
    pad = abuf.shape[0] - tt

    @pl.when(pl.program_id(0) % tiles_per_seq == 0)
    def _():
        carry_ref[...] = jnp.zeros_like(carry_ref)
        zbuf[0:o, :] = jnp.zeros((o, dc), F32)
        xbuf[0:o, :] = jnp.zeros((o, dl), F32)
        h_ref[...] = jnp.zeros_like(h_ref)
        abuf[0:pad, :] = jnp.ones((pad, dl), F32)
        ubuf[0:pad, :] = jnp.zeros((pad, dl), F32)

    xn = _rms(x_ref[...], g_ref[...]).astype(BF16)

    def proj(w_ref, col, width):
        return jnp.dot(xn, w_ref[:, col:col + width], preferred_element_type=F32)

    def conv3():
        yc = cw_ref[CONV_WIDTH - 1:CONV_WIDTH, :] * zbuf[o:o + tt, :]
        for s in range(1, CONV_WIDTH):
            yc = yc + cw_ref[CONV_WIDTH - 1 - s:CONV_WIDTH - s, :] * zbuf[o - s:o - s + tt, :]
        yc_ref[...] = cb_ref[...] * yc
        zbuf[0:o, :] = zbuf[tt:tt + o, :]

    def conv4():
        xr = lw_ref[LRU_CONV_WIDTH - 1:LRU_CONV_WIDTH, :] * xbuf[o:o + tt, :]
        for s in range(1, LRU_CONV_WIDTH):
            xr = xr + lw_ref[LRU_CONV_WIDTH - 1 - s:LRU_CONV_WIDTH - s, :] * xbuf[o - s:o - s + tt, :]
        xr = xr + lb_ref[...]
        xbuf[0:o, :] = xbuf[tt:tt + o, :]
        xbuf[o:o + tt, :] = xr

    blk = dl // LRU_BLOCKS
    lam = lam_ref[...]
    softplus_neg_lam = jnp.maximum(-lam, 0.0) + jnp.log1p(jnp.exp(-jnp.abs(lam)))

    def lru_inputs(g):
        sl = slice(g * blk, (g + 1) * blk)
        xg = xbuf[o:o + tt, sl]
        xg_bf = xg.astype(BF16)
        gates = jnp.dot(xg_bf, wax_ref[g], preferred_element_type=F32)
        r = jax.nn.sigmoid(gates[:, :blk] + ba_ref[:, sl])
        ig = jax.nn.sigmoid(gates[:, blk:] + bx_ref[:, sl])
        log_a = (-LRU_C * r) * softplus_neg_lam[:, sl]
        th = jnp.tanh(log_a)
        mult = jnp.sqrt(-2.0 * th / (1.0 - th))
        abuf[pad:pad + tt, sl] = jnp.exp(log_a)
        ubuf[pad:pad + tt, sl] = mult * (ig * xg)

    def scan_step(s):
        a = abuf[pad:pad + tt, :]
        u = ubuf[pad:pad + tt, :]
        a_sh = abuf[pad - s:pad - s + tt, :]
        u_sh = ubuf[pad - s:pad - s + tt, :]
        ubuf[pad:pad + tt, :] = a * u_sh + u
        abuf[pad:pad + tt, :] = a * a_sh

    def finish_lru():
        hs = abuf[pad:pad + tt, :] * h_ref[0:1, :] + ubuf[pad:pad + tt, :]
        h_ref[...] = jnp.broadcast_to(hs[tt - 1:tt, :], h_ref.shape)
        yl_ref[...] = _gelu_tanh(lg_ref[...]) * hs

    def store(out, c, w, scale=None):
        def run(col):
            y = proj(wm_ref, col, w)
            out[:, c:c + w] = (y if scale is None else y * scale).astype(out.dtype)
        return run

    def store_cb(col):
        cb_ref[...] = proj(wm_ref, col, dc)

    def store_lg(col):
        lg_ref[...] = proj(wl_ref, 0, dl)

    zbuf[o:o + tt, :] = proj(wm_ref, dc, dc) * proj(wm_ref, 2 * dc, dc)
    xbuf[o:o + tt, :] = proj(wl_ref, dl, dl)
    top = scan_ref.shape[0] - tt
    rows = slice(top, top + tt)
    z = jnp.dot(xn, wf_ref[...], preferred_element_type=F32) + fb_ref[...]
    scan_ref[rows, :] = -(jnp.maximum(-z, 0.0) + jnp.log1p(jnp.exp(-jnp.abs(z))))

    def forget_decay():
        scan_ref[0:top, :] = jnp.zeros((top, scan_ref.shape[1]), F32)
        shift = 1
        while shift < tt:
            scan_ref[rows, :] = scan_ref[rows, :] + scan_ref[top - shift:top - shift + tt, :]
            shift *= 2
        cum = scan_ref[rows, :] + carry_ref[0:1, :]
        cum_ref[...] = cum
        carry_ref[...] = jnp.broadcast_to(cum[tt - 1:tt, :], carry_ref.shape)

    mxu_work = [(store_cb, 0), (store_lg, 0)]
    col = 3 * dc
    for out, scale in ((q_ref, q_scale), (k_ref, None), (v_ref, None)):
        for c in range(0, out.shape[1], chunk):
            w = min(chunk, out.shape[1] - c)
            mxu_work.append((store(out, c, w, scale), col + c))
        col += out.shape[1]
    scans = []
    s = 1
    while s < tt:
        scans.append(functools.partial(scan_step, s))
        s *= 2
    vpu_work = ([[conv4], [functools.partial(lru_inputs, 0)], [conv3, forget_decay]]
                + [[functools.partial(lru_inputs, g)] for g in range(1, LRU_BLOCKS)]
                + [scans[:len(scans) // 2], scans[len(scans) // 2:] + [finish_lru]])
    assert len(mxu_work) >= len(vpu_work)
    for n, (dot, wcol) in enumerate(mxu_work):
        dot(wcol)
        for piece in (vpu_work[n] if n < len(vpu_work) else []):
            piece()


def _row_slab_cast(w_stack, layer, steps):
    slab = (w_stack.shape[1] // steps, w_stack.shape[2])
    assert slab[0] * steps == w_stack.shape[1] and slab[0] % (2 * SUBLANES) == 0
    return (pl.BlockSpec((None,) + slab, lambda i: (layer, i, 0)), pl.BlockSpec(slab, lambda i: (i, 0)),
            jax.ShapeDtypeStruct(w_stack.shape[1:], BF16))


def _mix_in(x, gain, w_all, w_lru, w_f, f_bias, conv_w, lru_conv_w, lru_conv_b, w_ax, b_a, b_x, lam,
            layer, d_att, q_scale, seq, casts=(), tm=256, chunk=512):
    n, d = x.shape
    d_conv, d_lru, f_lanes = conv_w.shape[2], lam.shape[1], w_f.shape[2]
    main = 3 * d_conv + 3 * d_att
    widths = (d_conv, d_att, d_att, d_att, d_lru, f_lanes)
    dtypes = (F32, BF16, BF16, BF16, F32, F32)
    pad = tm // 2
    per_layer = lambda a: pl.BlockSpec((None,) + a.shape[1:], lambda i: (layer,) + (0,) * (a.ndim - 1))
    resident = lambda shape: pl.BlockSpec((None,) + shape, lambda i: (layer, 0, 0),
                                          pipeline_mode=pl.Buffered(1))
    rows3 = lambda a: a.reshape(a.shape[0], 1, a.shape[1])
    small = [conv_w, lru_conv_w, rows3(lru_conv_b), w_ax, rows3(b_a), rows3(b_x), rows3(lam)]
    cast_specs = [_row_slab_cast(w_stack, which, n // tm) for w_stack, which in casts]
    return pl.pallas_call(
        functools.partial(_mix_in_kernel, chunk=chunk, q_scale=q_scale, tiles_per_seq=seq // tm,
                          n_cast=len(casts)),
        grid=(n // tm,),
        in_specs=[
            pl.BlockSpec((tm, d), lambda i: (i, 0)),
            pl.BlockSpec((None, 1, d), lambda i: (layer, 0, 0)),
            resident((d, main)), resident(w_lru.shape[1:]), resident(w_f.shape[1:]),
            pl.BlockSpec((None, 1, f_lanes), lambda i: (layer, 0, 0)),
        ] + [per_layer(a) for a in small] + [spec[0] for spec in cast_specs],
        out_specs=([pl.BlockSpec((tm, wd), lambda i: (i, 0)) for wd in widths]
                   + [spec[1] for spec in cast_specs]),
        out_shape=([jax.ShapeDtypeStruct((n, wd), dt) for wd, dt in zip(widths, dtypes)]
                   + [spec[2] for spec in cast_specs]),
        scratch_shapes=[
            pltpu.VMEM((SUBLANES, f_lanes), F32),
            pltpu.VMEM((pad + tm, f_lanes), F32),
            pltpu.VMEM((tm, d_conv), F32),
            pltpu.VMEM((tm, d_lru), F32),
            pltpu.VMEM((tm + SUBLANES, d_conv), F32),
            pltpu.VMEM((tm + SUBLANES, d_lru), F32),
            pltpu.VMEM((pad + tm, d_lru), F32),
            pltpu.VMEM((pad + tm, d_lru), F32),
            pltpu.VMEM((SUBLANES, d_lru), F32),
        ],
        compiler_params=_params(1),
        name="mix_in",
    )(x, gain.reshape(gain.shape[0], 1, d), w_all, w_lru, w_f, rows3(f_bias), *small,
      *[w_stack for w_stack, _ in casts])


DECAY_COPIES = 6


def _decay_split(c_block):
    c = c_block * LOG2E
    hi = c.astype(BF16).astype(F32)
    rest = c - hi
    mid = rest.astype(BF16).astype(F32)
    return hi, mid, rest - mid


def _decay_columns(pieces, head, nh, on_query_side):
    hi, mid, lo = pieces
    lane = lax.broadcasted_iota(jnp.int32, hi.shape, 1)
    onehot = jnp.where(lane % nh == head, 1.0, 0.0)
    first = 0 if on_query_side else 3 * nh
    split = jnp.where(lane < first + nh, hi, jnp.where(lane < first + 2 * nh, mid, lo))
    if on_query_side:
        out = jnp.where(lane < 3 * nh, split, jnp.where(lane < 6 * nh, onehot, 0.0))
    else:
        out = jnp.where(lane < 3 * nh, onehot, jnp.where(lane < 6 * nh, -split, 0.0))
    return out.astype(BF16)


def _attn_kernel(q_ref, k_ref, v_ref, cq_ref, ck_ref, *rest, tq, nh, build_rows, n_cast):
    cast_src, o_ref, cast_dst = rest[:n_cast], rest[n_cast], rest[n_cast + 1:2 * n_cast + 1]
    kx_ref, s_ref = rest[-2:]
    _cast_pieces(cast_src, cast_dst)
    i = pl.program_id(2)
    dh = HEAD_DIM
    hp = q_ref.shape[1] // dh
    heads = [(pl.program_id(1) * hp + g, slice(g * dh, (g + 1) * dh)) for g in range(hp)]

    @pl.when(i == 0)
    def _():
        def build(c, carry):
            rows = pl.ds(pl.multiple_of(c * build_rows, build_rows), build_rows)
            pieces = _decay_split(ck_ref[rows, :])
            for g, (head, cols) in enumerate(heads):
                kx_ref[g, rows, 0:dh] = k_ref[rows, cols]
                kx_ref[g, rows, dh:2 * dh] = _decay_columns(pieces, head, nh, on_query_side=False)
            return carry
        lax.fori_loop(0, k_ref.shape[0] // build_rows, build, 0)

    def key_rows(j):
        return pl.ds(pl.multiple_of(j * tq, tq), tq)

    pieces = _decay_split(cq_ref[...])
    qx_t = []
    for g, (head, cols) in enumerate(heads):
        qx = jnp.concatenate([q_ref[:, cols].astype(F32),
                              _decay_columns(pieces, head, nh, on_query_side=True).astype(F32)], axis=1)
        qx_t.append(qx.T.astype(BF16))
        s_ref[0, g] = jnp.dot(kx_ref[g, key_rows(0), :], qx_t[g], preferred_element_type=F32)

    def score(j, slot):
        for g in range(hp):
            s_ref[slot, g] = jnp.dot(kx_ref[g, key_rows(j), :], qx_t[g], preferred_element_type=F32)

    def absorb(j, slot, carries, masked=False):
        probs, new = [], []
        for g in range(hp):
            m, l, acc = carries[g]
            s = s_ref[slot, g]
            if masked:
                r = lax.broadcasted_iota(jnp.int32, s.shape, 0)
                c = lax.broadcasted_iota(jnp.int32, s.shape, 1)
                s = jnp.where(r <= c, s, -jnp.inf)
            m_new = jnp.maximum(m, jnp.max(s, axis=0, keepdims=True))
            alpha = jnp.exp2(m - m_new)
            p = jnp.exp2(s - m_new)
            probs.append(p.astype(BF16))
            new.append((m_new, alpha * l + jnp.sum(p, axis=0, keepdims=True), alpha * acc))
        out = []
        for g, p in enumerate(probs):
            pv = lax.dot_general(v_ref[key_rows(j), heads[g][1]], p, (((0,), (0,)), ((), ())),
                                 preferred_element_type=F32)
            out.append((new[g][0], new[g][1], new[g][2] + pv))
        return tuple(out)

    def two_blocks(pair, carries):
        j = 2 * pair
        score(j + 1, 1)
        carries = absorb(j, 0, carries)
        score(j + 2, 0)
        return absorb(j + 1, 1, carries)

    def last_block(carries):
        return absorb(i, 0, carries, masked=True)

    def last_two_blocks(carries):
        score(i, 1)
        return absorb(i, 1, absorb(i - 1, 0, carries), masked=True)

    init = (jnp.full((1, tq), -jnp.inf, F32), jnp.zeros((1, tq), F32), jnp.zeros((dh, tq), F32))
    carries = lax.fori_loop(0, i // 2, two_blocks, (init,) * hp)
    carries = lax.cond(i % 2 == 0, last_block, last_two_blocks, carries)
    for g, (_, cols) in enumerate(heads):
        m, l, acc = carries[g]
        o_ref[:, cols] = (acc * (1.0 / l)).T


def _attention(q, k, v, cum, batch, seq, casts=(), tq=512, hp=4, build_rows=512):
    n, d_att = q.shape
    nh = d_att // HEAD_DIM
    nq = seq // tq
    ng = nh // hp
    width = hp * HEAD_DIM
    in_specs = [
        pl.BlockSpec((tq, width), lambda b, h, i: (b * nq + i, h)),
        pl.BlockSpec((seq, width), lambda b, h, i: (b, h)),
        pl.BlockSpec((seq, width), lambda b, h, i: (b, h)),
        pl.BlockSpec((tq, LANES), lambda b, h, i: (b * nq + i, 0)),
        pl.BlockSpec((seq, LANES), lambda b, h, i: (b, 0)),
    ]
    args = [q, k, v, cum, cum]
    out_specs = [pl.BlockSpec((tq, width), lambda b, h, i: (b * nq + i, h))]
    out_shape = [jax.ShapeDtypeStruct((n, d_att), F32)]
    for w_stack, which in casts:
        tile = (w_stack.shape[1] // (batch * ng), w_stack.shape[2] // nq)
        assert tile[0] % (2 * SUBLANES) == 0 and tile[1] % LANES == 0
        in_specs.append(pl.BlockSpec((None,) + tile, lambda b, h, i, which=which: (which, b * ng + h, i)))
        args.append(w_stack)
        out_specs.append(pl.BlockSpec(tile, lambda b, h, i: (b * ng + h, i)))
        out_shape.append(jax.ShapeDtypeStruct(w_stack.shape[1:], BF16))
    return pl.pallas_call(
        functools.partial(_attn_kernel, tq=tq, nh=nh, build_rows=build_rows, n_cast=len(casts)),
        grid=(batch, ng, nq),
        in_specs=in_specs,
        out_specs=out_specs,
        out_shape=out_shape,
        scratch_shapes=[pltpu.VMEM((hp, seq, 2 * HEAD_DIM), BF16), pltpu.VMEM((2, hp, tq, tq), F32)],
        compiler_params=_params(3),
        name="fox_attention",
    )(*args)


def _mix_out_kernel(x_ref, yc_ref, ya_ref, yl_ref, g_ref, w_ref, *rest, n_cast):
    cast_src, o_ref, cast_dst, y_ref = rest[:n_cast], rest[n_cast], rest[n_cast + 1:2 * n_cast + 1], rest[-1]
    _cast_pieces(cast_src, cast_dst)
    col = 0
    for src in (yc_ref, ya_ref, yl_ref):
        width = src.shape[1]
        y_ref[:, col:col + width] = _rms(src[...], g_ref[:, col:col + width]).astype(BF16)
        col += width
    o_ref[...] = x_ref[...] + jnp.dot(y_ref[...], w_ref[...], preferred_element_type=F32)


def _mix_out(x, yc, ya, yl, gain, w, layer, casts=(), tm=512):
    n, d = x.shape
    dm = w.shape[0]
    row = lambda i: (i, 0)
    in_specs = [
        pl.BlockSpec((tm, d), row),
        pl.BlockSpec((tm, yc.shape[1]), row),
        pl.BlockSpec((tm, ya.shape[1]), row),
        pl.BlockSpec((tm, yl.shape[1]), row),
        pl.BlockSpec((None, 1, dm), lambda i: (layer, 0, 0)),
        pl.BlockSpec(w.shape, lambda i: (0, 0), pipeline_mode=pl.Buffered(1)),
    ]
    args = [x, yc, ya, yl, gain.reshape(gain.shape[0], 1, dm), w]
    out_specs = [pl.BlockSpec((tm, d), row)]
    out_shape = [jax.ShapeDtypeStruct((n, d), F32)]
    for w_stack, which in casts:
        src_spec, dst_spec, dst_shape = _row_slab_cast(w_stack, which, n // tm)
        in_specs.append(src_spec)
        args.append(w_stack)
        out_specs.append(dst_spec)
        out_shape.append(dst_shape)
    return pl.pallas_call(
        functools.partial(_mix_out_kernel, n_cast=len(casts)),
        grid=(n // tm,),
        in_specs=in_specs,
        out_specs=out_specs,
        out_shape=out_shape,
        scratch_shapes=[pltpu.VMEM((tm, dm), BF16)],
        compiler_params=_params(1),
        name="mix_out",
    )(*args)


def kernel(x, norm_ffn1, ffn1_w_in, ffn1_w_out, norm_mix, mix_w_in, conv_w, fgate_b, lru_conv_w,
           lru_conv_b, lru_w_a, lru_b_a, lru_w_x, lru_b_x, lru_lambda, mix_out_norm, mix_w_out,
           norm_ffn2, ffn2_w_in, ffn2_w_out, final_norm):
    batch, seq, d = x.shape
    depth = norm_ffn1.shape[0]
    d_conv = conv_w.shape[2]
    d_lru = lru_lambda.shape[1]
    d_att = mix_w_out.shape[1] - d_conv - d_lru
    nh = d_att // HEAD_DIM
    assert nh == fgate_b.shape[1] and DECAY_COPIES * nh <= LANES

    f0 = 3 * d_conv + 3 * d_att
    f_pad = LANES - DECAY_COPIES * nh
    fbias = jnp.pad(jnp.tile(fgate_b, (1, DECAY_COPIES)), ((0, 0), (0, f_pad)))
    w_ax = jnp.concatenate([lru_w_a, lru_w_x], axis=3).astype(BF16)
    q_scale = LOG2E * HEAD_DIM ** -0.5

    w1_in, w1_out = ffn1_w_in[0].astype(BF16), ffn1_w_out[0].astype(BF16)
    w_mix_out = mix_w_out[0].astype(BF16)
    w_mix = mix_w_in.astype(BF16)
    w_mix_lru = w_mix[:, :, f0 + nh:]
    w_mix_f = jnp.pad(jnp.tile(w_mix[:, :, f0:f0 + nh], (1, 1, DECAY_COPIES)), ((0, 0), (0, 0), (0, f_pad)))

    h = x.reshape(batch * seq, d)
    for l in range(depth):
        more = l + 1 < depth
        h = _ffn(h, norm_ffn1, w1_in, w1_out, l)
        yc, q, k, v, yl, cum, w2_in, *w1_in_next = _mix_in(
            h, norm_mix, w_mix, w_mix_lru, w_mix_f, fbias, conv_w, lru_conv_w, lru_conv_b, w_ax, lru_b_a,
            lru_b_x, lru_lambda, l, d_att, q_scale, seq,
            casts=[(ffn2_w_in, l)] + ([(ffn1_w_in, l + 1)] if more else []))
        ya, w2_out, *w1_out_next = _attention(
            q, k, v, cum, batch, seq, casts=[(ffn2_w_out, l)] + ([(ffn1_w_out, l + 1)] if more else []))
        h, *w_mix_out_next = _mix_out(h, yc, ya, yl, mix_out_norm, w_mix_out, l,
                                      casts=[(mix_w_out, l + 1)] if more else [])
        h = _ffn(h, norm_ffn2, w2_in, w2_out, l, final_gain=None if more else final_norm)
        if more:
            w1_in, w1_out, w_mix_out = w1_in_next[0], w1_out_next[0], w_mix_out_next[0]
    return h.reshape(batch, seq, d)
```
